```python
import math
import jax
import jax.numpy as jnp
from jax import lax
import numpy as np

D_MODEL = 1024
BATCH = 32
SEQ = 2048
DEPTH = 1
DEC_BATCH = 128
DEC_SEQ = 4
PAST_LEN = 8192
PAGE_SIZE = 128

HEAD_DIM = 64
N_ATT_HEADS = 8
ATT_WIDTH = N_ATT_HEADS * HEAD_DIM
ATT_SCALE = HEAD_DIM ** -0.5
DILATED_GROUPS = ((128, 1), (512, 4), (2048, 16))
WINDOW_MAX = 2048
BAND_BLOCK = 128
ROPE_THETA = 10000.0
N_RET_HEADS = 4
RET_QK_DIM = 64
RET_V_DIM = 128
RET_QK_WIDTH = N_RET_HEADS * RET_QK_DIM
RET_V_WIDTH = N_RET_HEADS * RET_V_DIM
RET_CHUNK = 128
MIX_WIDTH = RET_V_WIDTH + ATT_WIDTH
IN_WIDTHS = (RET_QK_WIDTH, RET_QK_WIDTH, RET_V_WIDTH, RET_V_WIDTH, ATT_WIDTH, ATT_WIDTH, ATT_WIDTH)
IN_WIDTH = 2 * RET_QK_WIDTH + 2 * RET_V_WIDTH + 3 * ATT_WIDTH
D_FF = -(-8 * D_MODEL // (3 * 256)) * 256
N_MOD = 6
EPS = 1e-6

kernel_name = 'hymba_retention_dilated_attn_step'


def rms_norm(x, gain=None):
    xf = x.astype(jnp.float32)
    y = xf * lax.rsqrt(jnp.mean(xf * xf, axis=-1, keepdims=True) + EPS)
    if gain is not None:
        y = y * gain.astype(jnp.float32)
    return y.astype(x.dtype)


def apply_rotary(x, pos, inv_freq):
    ang = pos.astype(jnp.float32)[:, None] * inv_freq[None, :]
    cos = jnp.cos(ang)[:, None, :]
    sin = jnp.sin(ang)[:, None, :]
    x1, x2 = jnp.split(x.astype(jnp.float32), 2, axis=-1)
    return jnp.concatenate([x1 * cos - x2 * sin, x1 * sin + x2 * cos], axis=-1).astype(x.dtype)


def rope_inv_freq(dim):
    return ROPE_THETA ** (-jnp.arange(0, dim, 2, dtype=jnp.float32) / dim)


def retnet_inv_freq(dim):
    return ROPE_THETA ** (-jnp.linspace(0.0, 1.0, dim // 2, dtype=jnp.float32))


def retention_chunked(q, k, v, state0, chunk):
    b, s, h, _ = q.shape
    dv = v.shape[-1]
    n = s // chunk
    log_g = jnp.log1p(-(2.0 ** (-5.0 - jnp.arange(h, dtype=jnp.float32))))
    idx = jnp.arange(chunk, dtype=jnp.float32)
    diff = idx[:, None] - idx[None, :]
    intra = jnp.where(diff >= 0, jnp.exp(jnp.maximum(diff, 0.0) * log_g[:, None, None]), 0.0)
    xi = jnp.exp((idx + 1.0)[None, :] * log_g[:, None])
    zeta = jnp.exp((chunk - 1.0 - idx)[None, :] * log_g[:, None])
    g_chunk = jnp.exp(chunk * log_g)

    def to_chunks(t):
        return t.astype(jnp.float32).reshape(b, n, chunk, h, -1).transpose(1, 0, 3, 2, 4)

    def step(r, qkv):
        qc, kc, vc = qkv
        sc = jnp.einsum('bhid,bhjd->bhij', qc, kc) * intra
        o = (jnp.einsum('bhij,bhjv->bhiv', sc, vc)
             + jnp.einsum('bhid,bhdv->bhiv', qc, r) * xi[None, :, :, None])
        r = r * g_chunk[None, :, None, None] + jnp.einsum('bhjd,bhjv->bhdv', kc * zeta[None, :, :, None], vc)
        return r, o

    r, o = lax.scan(step, state0.astype(jnp.float32), (to_chunks(q), to_chunks(k), to_chunks(v)))
    o = o.transpose(1, 0, 3, 2, 4).reshape(b, s, h, dv)
    return o, r


def banded_window_attention(q, k, v, n_back):
    nn_, ln, h, dh = q.shape
    blk = BAND_BLOCK
    nb = -(-ln // blk)
    lp = nb * blk
    qb = jnp.pad(q, ((0, 0), (0, lp - ln), (0, 0), (0, 0))).reshape(nn_, nb, blk, h, dh)

    def band(t):
        tp = jnp.pad(t, ((0, 0), (blk, lp - ln), (0, 0), (0, 0)))
        return jnp.concatenate([tp[:, :lp].reshape(nn_, nb, blk, h, dh),
                                tp[:, blk:].reshape(nn_, nb, blk, h, dh)], axis=2)

    kb, vb = band(k), band(v)
    s = jnp.einsum('nbihd,nbjhd->nbhij', qb, kb, preferred_element_type=jnp.float32) * ATT_SCALE
    qi = jnp.arange(nb)[:, None] * blk + jnp.arange(blk)[None, :]
    kj = jnp.arange(nb)[:, None] * blk - blk + jnp.arange(2 * blk)[None, :]
    dist = qi[:, :, None] - kj[:, None, :]
    mask = (dist >= 0) & (dist <= n_back) & (kj[:, None, :] >= 0)
    s = jnp.where(mask[None, :, None], s, -jnp.inf)
    lse = jax.nn.logsumexp(s, axis=-1)
    p = jnp.exp(s - lse[..., None])
    o = jnp.einsum('nbhij,nbjhd->nbihd', p, vb.astype(jnp.float32))
    o = o.reshape(nn_, lp, h, dh)[:, :ln]
    lse = lse.transpose(0, 1, 3, 2).reshape(nn_, lp, h)[:, :ln]
    return o, lse


def combine_by_denominator(outs, lses):
    w = jax.nn.softmax(jnp.stack(lses, axis=0), axis=0)
    return jnp.sum(w[..., None] * jnp.stack(outs, axis=0), axis=0)


def dilated_attention_prompt(q, k, v):
    b, s, h, dh = q.shape
    outs, lses = [], []
    for window, dil in DILATED_GROUPS:
        ln = s // dil

        def split(t):
            return t.reshape(b, ln, dil, h, dh).transpose(0, 2, 1, 3, 4).reshape(b * dil, ln, h, dh)

        o, lse = banded_window_attention(split(q), split(k), split(v), window // dil)
        outs.append(o.reshape(b, dil, ln, h, dh).transpose(0, 2, 1, 3, 4).reshape(b, s, h, dh))
        lses.append(lse.reshape(b, dil, ln, h).transpose(0, 2, 1, 3).reshape(b, s, h))
    return combine_by_denominator(outs, lses)


def dilated_attention_sample(q, k_all, v_all, buf_len):
    b, t, h, dh = q.shape
    outs, lses = [], []
    for window, dil in DILATED_GROUPS:
        n = window // dil
        idx = buf_len + jnp.arange(t)[:, None] - dil * jnp.arange(n + 1)[None, :]
        valid = idx >= 0
        flat = jnp.clip(idx, 0).reshape(-1)
        kg = jnp.take(k_all, flat, axis=1).reshape(b, t, n + 1, h, dh)
        vg = jnp.take(v_all, flat, axis=1).reshape(b, t, n + 1, h, dh)
        s = jnp.einsum('bthd,btmhd->bhtm', q, kg, preferred_element_type=jnp.float32) * ATT_SCALE
        s = jnp.where(valid[None, None], s, -jnp.inf)
        lse = jax.nn.logsumexp(s, axis=-1)
        p = jnp.exp(s - lse[..., None])
        outs.append(jnp.einsum('bhtm,btmhd->bthd', p, vg.astype(jnp.float32)))
        lses.append(lse.transpose(0, 2, 1))
    return combine_by_denominator(outs, lses)


def hybrid_layer(x, c, pos, ret_state0, attend, w_ada, b_ada, norm1_g, w_in, w_out,
                 norm2_g, w_gate, w_up, w_down):
    b, s, _ = x.shape
    mod = jnp.dot(jax.nn.silu(c), w_ada) + b_ada
    sh1, sc1, g1, sh2, sc2, g2 = jnp.split(mod[:, None, :], N_MOD, axis=-1)
    h = rms_norm(x, norm1_g) * (1.0 + sc1) + sh1
    z = jnp.einsum('bsd,de->bse', h, w_in)
    offs = [int(o) for o in np.cumsum(IN_WIDTHS)[:-1]]
    rq, rk, rv, rg, aq, ak, av = jnp.split(z, offs, axis=-1)
    rfreq = retnet_inv_freq(RET_QK_DIM)
    rq = apply_rotary(rq.reshape(b, s, N_RET_HEADS, RET_QK_DIM), pos, rfreq)
    rk = apply_rotary(rk.reshape(b, s, N_RET_HEADS, RET_QK_DIM), pos, rfreq) * (RET_QK_DIM ** -0.5)
    rv = rv.reshape(b, s, N_RET_HEADS, RET_V_DIM)
    ret_o, ret_state = retention_chunked(rq, rk, rv, ret_state0, math.gcd(s, RET_CHUNK))
    ret_y = rms_norm(ret_o).reshape(b, s, RET_V_WIDTH) * jax.nn.silu(rg.astype(jnp.float32))
    afreq = rope_inv_freq(HEAD_DIM)
    aq = apply_rotary(aq.reshape(b, s, N_ATT_HEADS, HEAD_DIM), pos, afreq)
    ak = apply_rotary(ak.reshape(b, s, N_ATT_HEADS, HEAD_DIM), pos, afreq)
    av = av.reshape(b, s, N_ATT_HEADS, HEAD_DIM)
    att_y = attend(aq, ak, av).reshape(b, s, ATT_WIDTH)
    mixed = jnp.concatenate([ret_y, att_y], axis=-1).astype(x.dtype)
    x = x + g1 * jnp.einsum('bse,ed->bsd', mixed, w_out)
    h = rms_norm(x, norm2_g) * (1.0 + sc2) + sh2
    ff = jax.nn.silu(jnp.einsum('bsd,df->bsf', h, w_gate)) * jnp.einsum('bsd,df->bsf', h, w_up)
    x = x + g2 * jnp.einsum('bsf,fd->bsd', ff, w_down)
    return x, ret_state.astype(x.dtype), ak, av


def setup_inputs(seed: int = 0) -> dict:
    key = jax.random.key(seed)
    ks = jax.random.split(key, 18)
    buf = min(WINDOW_MAX, PAST_LEN)

    def nrm(k, shape, scale):
        return jax.random.normal(k, shape, jnp.float32) * scale

    return {
        'x_prompt': nrm(ks[0], (BATCH, SEQ, D_MODEL), 1.0),
        'x_sample': nrm(ks[1], (DEC_BATCH, DEC_SEQ, D_MODEL), 1.0),
        'cache_attn_k': nrm(ks[2], (DEPTH, DEC_BATCH, buf, N_ATT_HEADS, HEAD_DIM), 1.0),
        'cache_attn_v': nrm(ks[3], (DEPTH, DEC_BATCH, buf, N_ATT_HEADS, HEAD_DIM), 1.0),
        'state_ret': nrm(ks[4], (DEPTH, DEC_BATCH, N_RET_HEADS, RET_QK_DIM, RET_V_DIM), 1.0),
        'c_prompt': nrm(ks[5], (BATCH, D_MODEL), 1.0),
        'c_sample': nrm(ks[6], (DEC_BATCH, D_MODEL), 1.0),
        'w_ada': nrm(ks[7], (DEPTH, D_MODEL, N_MOD * D_MODEL), 0.5 * D_MODEL ** -0.5),
        'b_ada': nrm(ks[8], (DEPTH, N_MOD * D_MODEL), 0.02),
        'norm1_g': 1.0 + nrm(ks[9], (DEPTH, D_MODEL), 0.02),
        'w_in': nrm(ks[10], (DEPTH, D_MODEL, IN_WIDTH), D_MODEL ** -0.5),
        'w_out': nrm(ks[11], (DEPTH, MIX_WIDTH, D_MODEL), MIX_WIDTH ** -0.5),
        'norm2_g': 1.0 + nrm(ks[12], (DEPTH, D_MODEL), 0.02),
        'w_gate': nrm(ks[13], (DEPTH, D_MODEL, D_FF), D_MODEL ** -0.5),
        'w_up': nrm(ks[14], (DEPTH, D_MODEL, D_FF), D_MODEL ** -0.5),
        'w_down': nrm(ks[15], (DEPTH, D_FF, D_MODEL), D_FF ** -0.5),
        'final_g': 1.0 + nrm(ks[16], (D_MODEL,), 0.02),
    }


def reference(x_prompt, x_sample, cache_attn_k, cache_attn_v, state_ret, c_prompt, c_sample,
              w_ada, b_ada, norm1_g, w_in, w_out, norm2_g, w_gate, w_up, w_down, final_g):
    seq = x_prompt.shape[1]
    dec_seq = x_sample.shape[1]
    pos_p = jnp.arange(seq)
    pos_s = PAST_LEN + jnp.arange(dec_seq)
    win_p = min(WINDOW_MAX, seq)
    xp, xs = x_prompt, x_sample
    kp_l, vp_l, rp_l, ks_l, vs_l, rs_l = [], [], [], [], [], []
    for l in range(DEPTH):
        lw = (w_ada[l], b_ada[l], norm1_g[l], w_in[l], w_out[l], norm2_g[l], w_gate[l], w_up[l], w_down[l])
        ret0 = jnp.zeros((xp.shape[0], N_RET_HEADS, RET_QK_DIM, RET_V_DIM), jnp.float32)
        xp, rp, kp, vp = hybrid_layer(xp, c_prompt, pos_p, ret0, dilated_attention_prompt, *lw)
        kp_l.append(kp[:, seq - win_p:])
        vp_l.append(vp[:, seq - win_p:])
        rp_l.append(rp)
        k_past, v_past = cache_attn_k[l], cache_attn_v[l]

        def attend_sample(q, k, v, k_past=k_past, v_past=v_past):
            return dilated_attention_sample(q, jnp.concatenate([k_past, k], axis=1),
                                            jnp.concatenate([v_past, v], axis=1), k_past.shape[1])

        xs, rs, ks_, vs_ = hybrid_layer(xs, c_sample, pos_s, state_ret[l], attend_sample, *lw)
        ks_l.append(ks_)
        vs_l.append(vs_)
        rs_l.append(rs.astype(state_ret.dtype))
    y_prompt = rms_norm(xp, final_g)
    y_sample = rms_norm(xs, final_g)
    new_k_prompt = jnp.stack(kp_l, axis=0)
    new_v_prompt = jnp.stack(vp_l, axis=0)
    new_ret_prompt = jnp.stack(rp_l, axis=0)
    new_k_sample = jnp.stack(ks_l, axis=0)
    new_v_sample = jnp.stack(vs_l, axis=0)
    new_ret_sample = jnp.stack(rs_l, axis=0)
    return (y_prompt, y_sample, new_k_prompt, new_v_prompt, new_ret_prompt, new_k_sample, new_v_sample, new_ret_sample)
```

```python
import functools

import jax
import jax.numpy as jnp
import numpy as np
from jax import lax
from jax.experimental import pallas as pl
from jax.experimental.pallas import tpu as pltpu

D_MODEL = 1024
HEAD_DIM = 64
N_ATT_HEADS = 8
ATT_WIDTH = N_ATT_HEADS * HEAD_DIM
ATT_SCALE = HEAD_DIM ** -0.5
DILATIONS = (16, 4, 1)
N_BACK = 128
ROPE_THETA = 10000.0
N_RET_HEADS = 4
RET_QK_DIM = 64
RET_V_DIM = 128
RET_QK_WIDTH = N_RET_HEADS * RET_QK_DIM
RET_V_WIDTH = N_RET_HEADS * RET_V_DIM
RET_CHUNK = 128
D_FF = 2816
N_MOD = 6
EPS = 1e-6
PAST_LEN = 8192

LANES = 128
BLK = 128
NEG = -1e30
VMEM_LIMIT = 56 * 1024 * 1024
BF16 = jnp.bfloat16
F32 = jnp.float32

_NT = (((1,), (1,)), ((), ()))
_TN = (((0,), (0,)), ((), ()))


def _cparams(*sem):
    return pltpu.CompilerParams(dimension_semantics=sem, vmem_limit_bytes=VMEM_LIMIT)


def _const_spec(shape):
    n = len(shape)
    return pl.BlockSpec(shape, lambda *_: (0,) * n, pipeline_mode=pl.Buffered(1))


def _head_mask(hh, rows, dtype):
    lane = lax.broadcasted_iota(jnp.int32, (rows, LANES), 1)
    own = (lane >= HEAD_DIM) if hh else (lane < HEAD_DIM)
    return jnp.where(own, 1.0, 0.0).astype(dtype)


def _rms(x):
    return x * lax.rsqrt(jnp.mean(x * x, axis=-1, keepdims=True) + EPS)


def _silu(x):
    return x * jax.nn.sigmoid(x)


def _ada_kernel(c_ref, w_ref, b_ref, o_ref):
    c = _silu(c_ref[...]).astype(BF16)
    o_ref[...] = jnp.dot(c, w_ref[...].astype(BF16), preferred_element_type=F32) + b_ref[...]


def _ada(c_all, w_ada, b_ada):
    n, d = c_all.shape
    width = w_ada.shape[1]
    tn = 1536
    return pl.pallas_call(
        _ada_kernel,
        grid=(width // tn,),
        in_specs=[pl.BlockSpec((n, d), lambda j: (0, 0)),
                  pl.BlockSpec((d, tn), lambda j: (0, j)),
                  pl.BlockSpec((1, tn), lambda j: (0, j))],
        out_specs=pl.BlockSpec((n, tn), lambda j: (0, j)),
        out_shape=jax.ShapeDtypeStruct((n, width), F32),
        compiler_params=_cparams("arbitrary"),
        name="ada_mod",
    )(c_all, w_ada, b_ada.reshape(1, width))


def _rotate(z, cos, sin_signed):
    lane = lax.broadcasted_iota(jnp.int32, (z.shape[0], LANES), 1)
    first_half = (lane & (HEAD_DIM - 1)) < (HEAD_DIM // 2)
    outs = []
    for c in range(z.shape[1] // LANES):
        zc = z[:, c * LANES:(c + 1) * LANES]
        partner = jnp.where(first_half, pltpu.roll(zc, LANES - HEAD_DIM // 2, 1), pltpu.roll(zc, HEAD_DIM // 2, 1))
        outs.append(zc * cos + partner * sin_signed)
    return jnp.concatenate(outs, axis=1)


_IN_OFFS = tuple(int(o) for o in np.cumsum((0, RET_QK_WIDTH, RET_QK_WIDTH, RET_V_WIDTH, RET_V_WIDTH,
                                            ATT_WIDTH, ATT_WIDTH, ATT_WIDTH)))


def _inproj_kernel(x_ref, sh_ref, sc_ref, g_ref, w_ref, cr_ref, sr_ref, ca_ref, sa_ref,
                   rq_ref, rk_ref, rv_ref, rg_ref, aq_ref, akb_ref, avb_ref, ak_ref, av_ref):
    h = _rms(x_ref[...]) * g_ref[...]
    h = (h * (1.0 + sc_ref[...]) + sh_ref[...]).astype(BF16)

    def proj(i):
        return jnp.dot(h, w_ref[:, _IN_OFFS[i]:_IN_OFFS[i + 1]], preferred_element_type=F32)

    cr, sr, ca, sa = cr_ref[...], sr_ref[...], ca_ref[...], sa_ref[...]
    rq_ref[...] = _rotate(proj(0), cr, sr).astype(BF16)
    rk_ref[...] = (_rotate(proj(1), cr, sr) * (RET_QK_DIM ** -0.5)).astype(BF16)
    rv_ref[...] = proj(2).astype(BF16)
    rg_ref[...] = proj(3).astype(BF16)
    aq_ref[...] = (_rotate(proj(4), ca, sa) * ATT_SCALE).astype(BF16)
    ak = _rotate(proj(5), ca, sa)
    ak_ref[...] = ak
    akb_ref[...] = ak.astype(BF16)
    av = proj(6)
    av_ref[...] = av
    avb_ref[...] = av.astype(BF16)


def _inproj(x2, mod_spec, mod_arr, g1, w_in_b, tabs, tm, tab_map):
    t, d = x2.shape
    row = lambda w: pl.BlockSpec((tm, w), lambda i: (i, 0))
    tab = pl.BlockSpec((tm, LANES), tab_map)
    widths = (RET_QK_WIDTH, RET_QK_WIDTH, RET_V_WIDTH, RET_V_WIDTH, ATT_WIDTH, ATT_WIDTH, ATT_WIDTH,
              ATT_WIDTH, ATT_WIDTH)
    dtypes = (BF16,) * 7 + (F32, F32)
    return pl.pallas_call(
        _inproj_kernel,
        grid=(t // tm,),
        in_specs=[row(d), mod_spec(0), mod_spec(1), _const_spec((1, d)), _const_spec(w_in_b.shape),
                  tab, tab, tab, tab],
        out_specs=[row(w) for w in widths],
        out_shape=[jax.ShapeDtypeStruct((t, w), dt) for w, dt in zip(widths, dtypes)],
        compiler_params=_cparams("arbitrary"),
        name="inproj_rope",
    )(x2, mod_arr, mod_arr, g1, w_in_b, *tabs)


def _ret_prompt_kernel(q_ref, k_ref, v_ref, g_ref, intra_ref, xi_ref, zeta_ref, gdec_ref,
                       y_ref, st_ref, r_ref):
    n_chunks = q_ref.shape[0] // RET_CHUNK
    r_ref[...] = jnp.zeros_like(r_ref)

    def chunk(c, carry):
        rows = pl.ds(pl.multiple_of(c * RET_CHUNK, RET_CHUNK), RET_CHUNK)
        for p in range(N_RET_HEADS // 2):
            pc = slice(p * LANES, (p + 1) * LANES)
            qp = q_ref[rows, pc]
            kp = k_ref[rows, pc]
            r_old = r_ref[p]
            r_new = r_old * gdec_ref[p]
            r_b = r_old.astype(BF16)
            for hh in range(2):
                h = 2 * p + hh
                hc = slice(h * RET_V_DIM, (h + 1) * RET_V_DIM)
                qm = qp * _head_mask(hh, RET_CHUNK, BF16)
                s = lax.dot_general(qm, kp, _NT, preferred_element_type=F32) * intra_ref[h]
                vh = v_ref[rows, hc]
                o = jnp.dot(s.astype(BF16), vh, preferred_element_type=F32)
                o = o + jnp.dot(qm, r_b, preferred_element_type=F32) * xi_ref[h]
                y_ref[rows, hc] = (_rms(o) * _silu(g_ref[rows, hc].astype(F32))).astype(BF16)
                kz = (kp.astype(F32) * zeta_ref[h] * _head_mask(hh, RET_CHUNK, F32)).astype(BF16)
                r_new = r_new + lax.dot_general(kz, vh, _TN, preferred_element_type=F32)
            r_ref[p] = r_new
        return carry

    lax.fori_loop(0, n_chunks, chunk, 0)
    st_ref[...] = r_ref[...]


def _ret_tables(chunk, reps):
    h = N_RET_HEADS
    log_g = jnp.log1p(-(2.0 ** (-5.0 - jnp.arange(h, dtype=F32))))
    idx = jnp.arange(chunk, dtype=F32)
    diff = idx[:, None] - idx[None, :]
    intra = jnp.where(diff >= 0, jnp.exp(jnp.maximum(diff, 0.0) * log_g[:, None, None]), 0.0)
    xi = jnp.exp((idx + 1.0)[None, :] * log_g[:, None])
    zeta = jnp.exp((chunk - 1.0 - idx)[None, :] * log_g[:, None])
    g_chunk = jnp.exp(chunk * log_g)
    eye = jnp.eye(reps, dtype=F32)
    intra_bd = jnp.einsum("ab,hij->haibj", eye, intra).reshape(h, reps * chunk, reps * chunk)
    n = reps * chunk
    xi_t = jnp.broadcast_to(jnp.tile(xi, (1, reps))[:, :, None], (h, n, RET_V_DIM))
    zeta_t = jnp.broadcast_to(jnp.tile(zeta, (1, reps))[:, :, None], (h, n, LANES))
    gdec = jnp.broadcast_to(jnp.repeat(g_chunk, RET_QK_DIM).reshape(h // 2, LANES, 1), (h // 2, LANES, RET_V_DIM))
    return intra_bd, xi_t, zeta_t, gdec


def _ret_prompt(rq, rk, rv, rg, b, s):
    tabs = _ret_tables(RET_CHUNK, 1)
    seq = lambda w: pl.BlockSpec((s, w), lambda i: (i, 0))
    return pl.pallas_call(
        _ret_prompt_kernel,
        grid=(b,),
        in_specs=[seq(RET_QK_WIDTH), seq(RET_QK_WIDTH), seq(RET_V_WIDTH), seq(RET_V_WIDTH)]
                 + [_const_spec(t.shape) for t in tabs],
        out_specs=[seq(RET_V_WIDTH),
                   pl.BlockSpec((None, N_RET_HEADS // 2, LANES, RET_V_DIM), lambda i: (i, 0, 0, 0))],
        out_shape=[jax.ShapeDtypeStruct((b * s, RET_V_WIDTH), BF16),
                   jax.ShapeDtypeStruct((b, N_RET_HEADS // 2, LANES, RET_V_DIM), F32)],
        scratch_shapes=[pltpu.VMEM((N_RET_HEADS // 2, LANES, RET_V_DIM), F32)],
        compiler_params=_cparams("arbitrary"),
        name="ret_prompt",
    )(rq, rk, rv, rg, *tabs)


SAMPLE_GROUP = 32


def _ret_sample_kernel(q_ref, k_ref, v_ref, g_ref, st_ref, intra_ref, xi_ref, zeta_ref, gdec_ref,
                       y_ref, so_ref, *, dec_seq):
    n = q_ref.shape[0]
    nb = n // dec_seq
    tok_seq = lax.broadcasted_iota(jnp.int32, (n, LANES), 0) // dec_seq
    seq_of_col = lax.broadcasted_iota(jnp.int32, (LANES, n), 1) // dec_seq
    for p in range(N_RET_HEADS // 2):
        pc = slice(p * LANES, (p + 1) * LANES)
        qp = q_ref[:, pc]
        kp = k_ref[:, pc]
        st = st_ref[:, p]
        st_flat = st.reshape(nb * LANES, RET_V_DIM).astype(BF16)
        upd = jnp.zeros((nb * LANES, RET_V_DIM), F32)
        for hh in range(2):
            h = 2 * p + hh
            hc = slice(h * RET_V_DIM, (h + 1) * RET_V_DIM)
            qm = qp * _head_mask(hh, n, BF16)
            s = lax.dot_general(qm, kp, _NT, preferred_element_type=F32) * intra_ref[h]
            vh = v_ref[:, hc]
            o = jnp.dot(s.astype(BF16), vh, preferred_element_type=F32)
            qf = qm.astype(F32)
            qexp = jnp.concatenate([jnp.where(tok_seq == b, qf, 0.0) for b in range(nb)], axis=1).astype(BF16)
            o = o + jnp.dot(qexp, st_flat, preferred_element_type=F32) * xi_ref[h]
            y_ref[:, hc] = (_rms(o) * _silu(g_ref[:, hc].astype(F32))).astype(BF16)
            kzt = (kp.astype(F32) * zeta_ref[h] * _head_mask(hh, n, F32)).T
            w = jnp.concatenate([jnp.where(seq_of_col == b, kzt, 0.0) for b in range(nb)], axis=0).astype(BF16)
            upd = upd + jnp.dot(w, vh, preferred_element_type=F32)
        so_ref[:, p] = st * gdec_ref[p] + upd.reshape(nb, LANES, RET_V_DIM)


def _ret_sample(rq, rk, rv, rg, state, dec_seq):
    b = state.shape[0]
    grp = SAMPLE_GROUP
    n = grp * dec_seq
    tabs = _ret_tables(dec_seq, grp)
    tok = lambda w: pl.BlockSpec((n, w), lambda i: (i, 0))
    st_spec = pl.BlockSpec((grp, N_RET_HEADS // 2, LANES, RET_V_DIM), lambda i: (i, 0, 0, 0))
    return pl.pallas_call(
        functools.partial(_ret_sample_kernel, dec_seq=dec_seq),
        grid=(b // grp,),
        in_specs=[tok(RET_QK_WIDTH), tok(RET_QK_WIDTH), tok(RET_V_WIDTH), tok(RET_V_WIDTH), st_spec]
                 + [_const_spec(t.shape) for t in tabs],
        out_specs=[tok(RET_V_WIDTH), st_spec],
        out_shape=[jax.ShapeDtypeStruct((b * dec_seq, RET_V_WIDTH), BF16),
                   jax.ShapeDtypeStruct(state.shape, F32)],
        compiler_params=_cparams("arbitrary"),
        name="ret_sample",
    )(rq, rk, rv, rg, state, *tabs)


def _attn_prompt_kernel(*refs, seq):
    n_pat = len(DILATIONS)
    in_refs = refs[:5 * n_pat]
    o_ref = refs[5 * n_pat]
    acc, m_s, l_s = refs[5 * n_pat + 1:]
    t = pl.program_id(1)
    n_pairs = N_ATT_HEADS // 2

    @pl.when(t == 0)
    def _():
        m_s[...] = jnp.full(m_s.shape, NEG, F32)
        l_s[...] = jnp.zeros(l_s.shape, F32)
        acc[...] = jnp.zeros(acc.shape, F32)

    row = lax.broadcasted_iota(jnp.int32, (BLK, 2 * BLK), 0)
    col = lax.broadcasted_iota(jnp.int32, (BLK, 2 * BLK), 1)
    band = (col >= row) & (col <= row + N_BACK)
    lo = lax.broadcasted_iota(jnp.int32, (BLK, LANES), 1) < HEAD_DIM

    for pi, d in enumerate(DILATIONS):
        q_ref, kp_ref, kc_ref, vp_ref, vc_ref = in_refs[5 * pi:5 * pi + 5]
        nb = seq // d // BLK
        i = t % nb
        r = t // nb
        valid = band & (col >= jnp.where(i > 0, 0, BLK))
        if d == 1:
            nat = pl.ds(pl.multiple_of(t * BLK, BLK), BLK)
        else:
            nat = pl.ds(r + d * BLK * i, BLK, stride=d)
        for p in range(n_pairs):
            pc = slice(p * LANES, (p + 1) * LANES)
            qp = q_ref[:, pc]
            kw = jnp.concatenate([kp_ref[:, pc], kc_ref[:, pc]], axis=0)
            vw = jnp.concatenate([vp_ref[:, pc], vc_ref[:, pc]], axis=0)
            stats = []
            for hh in range(2):
                s = lax.dot_general(qp * _head_mask(hh, BLK, BF16), kw, _NT, preferred_element_type=F32)
                s = jnp.where(valid, s, NEG)
                m = jnp.max(s, axis=-1, keepdims=True)
                e = jnp.exp(s - m)
                l = jnp.sum(e, axis=-1, keepdims=True)
                o = jnp.dot(e.astype(BF16), vw, preferred_element_type=F32)
                stats.append((m, l, o))
            m_p = jnp.where(lo, stats[0][0], stats[1][0])
            l_p = jnp.where(lo, stats[0][1], stats[1][1])
            o_p = jnp.where(lo, stats[0][2], stats[1][2])
            m_o = m_s[p, nat, :]
            m_n = jnp.maximum(m_o, m_p)
            a_o = jnp.exp(m_o - m_n)
            a_p = jnp.exp(m_p - m_n)
            m_s[p, nat, :] = m_n
            l_s[p, nat, :] = a_o * l_s[p, nat, :] + a_p * l_p
            acc[p, nat, :] = a_o * acc[p, nat, :] + a_p * o_p

    @pl.when(t == pl.num_programs(1) - 1)
    def _():
        for p in range(n_pairs):
            o_ref[:, p * LANES:(p + 1) * LANES] = (acc[p] / l_s[p]).astype(BF16)


def _attn_prompt(aq, ak, av, b, s):
    n_blk = s // BLK
    args, specs = [], []
    for d in DILATIONS:
        nb = s // d // BLK
        view = lambda a, d=d: a.reshape(b, s // d, d * ATT_WIDTH)
        shape = (None, BLK, ATT_WIDTH)
        cur = lambda bi, t, nb=nb: (bi, t % nb, t // nb)
        prev = lambda bi, t, nb=nb: (bi, jnp.maximum(t % nb - 1, 0), t // nb)
        args += [view(aq), view(ak), view(ak), view(av), view(av)]
        specs += [pl.BlockSpec(shape, cur), pl.BlockSpec(shape, prev), pl.BlockSpec(shape, cur),
                  pl.BlockSpec(shape, prev), pl.BlockSpec(shape, cur)]
    slab = pltpu.VMEM((N_ATT_HEADS // 2, s, LANES), F32)
    return pl.pallas_call(
        functools.partial(_attn_prompt_kernel, seq=s),
        grid=(b, n_blk),
        in_specs=specs,
        out_specs=pl.BlockSpec((s, ATT_WIDTH), lambda bi, t: (bi, 0)),
        out_shape=jax.ShapeDtypeStruct((b * s, ATT_WIDTH), BF16),
        scratch_shapes=[slab, slab, slab],
        compiler_params=_cparams("arbitrary", "arbitrary"),
        name="attn_prompt",
    )(*args)


def _attn_sample_kernel(q_ref, kn_ref, vn_ref, kc_ref, vc_ref, o_ref, k_all, v_all, *, dec_seq, buf):
    n_all = k_all.shape[0]
    tail = 16
    k_all[0:buf, :] = kc_ref[...].astype(BF16)
    v_all[0:buf, :] = vc_ref[...].astype(BF16)
    trow = lax.broadcasted_iota(jnp.int32, (tail, ATT_WIDTH), 0)
    kt = jnp.zeros((tail, ATT_WIDTH), F32)
    vt = jnp.zeros((tail, ATT_WIDTH), F32)
    for t in range(dec_seq):
        kt = jnp.where(trow == t, kn_ref[t:t + 1, :], kt)
        vt = jnp.where(trow == t, vn_ref[t:t + 1, :], vt)
    k_all[buf:buf + tail, :] = kt.astype(BF16)
    v_all[buf:buf + tail, :] = vt.astype(BF16)
    k_all[buf + tail:n_all, :] = jnp.zeros((n_all - buf - tail, ATT_WIDTH), BF16)
    v_all[buf + tail:n_all, :] = jnp.zeros((n_all - buf - tail, ATT_WIDTH), BF16)

    hrow = lax.broadcasted_iota(jnp.int32, (N_ATT_HEADS, ATT_WIDTH), 0)
    hlane = lax.broadcasted_iota(jnp.int32, (N_ATT_HEADS, ATT_WIDTH), 1) // HEAD_DIM
    own = jnp.where(hrow == hlane, 1.0, 0.0)
    qb = jnp.concatenate([q_ref[t:t + 1, :] * own for t in range(dec_seq)], axis=0).astype(BF16)
    s = lax.dot_general(qb, k_all[...], _NT, preferred_element_type=F32)

    n_rows = dec_seq * N_ATT_HEADS
    tq = lax.broadcasted_iota(jnp.int32, (n_rows, n_all), 0) // N_ATT_HEADS
    j = lax.broadcasted_iota(jnp.int32, (n_rows, n_all), 1)
    dist = buf + tq - j
    cnt = jnp.zeros((n_rows, n_all), F32)
    for d in DILATIONS:
        hit = (dist >= 0) & (dist <= d * N_BACK) & ((dist & (d - 1)) == 0)
        cnt = cnt + jnp.where(hit, 1.0, 0.0)
    s = jnp.where(cnt > 0.0, s, NEG)
    m = jnp.max(s, axis=-1, keepdims=True)
    e = cnt * jnp.exp(s - m)
    l = jnp.sum(e, axis=-1, keepdims=True)
    o = jnp.dot(e.astype(BF16), v_all[...], preferred_element_type=F32) / l
    for t in range(dec_seq):
        o_ref[t:t + 1, :] = jnp.sum(o[t * N_ATT_HEADS:(t + 1) * N_ATT_HEADS, :] * own, axis=0, keepdims=True)


def _attn_sample(aq, ak, av, cache_k, cache_v, dec_seq):
    b, buf, _ = cache_k.shape
    n_all = buf + LANES
    tokspec = pl.BlockSpec((None, dec_seq, ATT_WIDTH), lambda i: (i, 0, 0))
    cspec = pl.BlockSpec((None, buf, ATT_WIDTH), lambda i: (i, 0, 0))
    return pl.pallas_call(
        functools.partial(_attn_sample_kernel, dec_seq=dec_seq, buf=buf),
        grid=(b,),
        in_specs=[tokspec, tokspec, tokspec, cspec, cspec],
        out_specs=tokspec,
        out_shape=jax.ShapeDtypeStruct((b, dec_seq, ATT_WIDTH), F32),
        scratch_shapes=[pltpu.VMEM((n_all, ATT_WIDTH), BF16), pltpu.VMEM((n_all, ATT_WIDTH), BF16)],
        compiler_params=_cparams("arbitrary"),
        name="attn_sample",
    )(aq, ak, av, cache_k, cache_v)


FF_CHUNK = 256


def _ffn_kernel(x_ref, ry_ref, ay_ref, g1_ref, sh2_ref, sc2_ref, g2_ref, n2_ref, fg_ref,
                wo_ref, wg_ref, wu_ref, wd_ref, y_ref):
    mix = jnp.dot(ry_ref[...], wo_ref[0:RET_V_WIDTH, :], preferred_element_type=F32)
    mix = mix + jnp.dot(ay_ref[...], wo_ref[RET_V_WIDTH:, :], preferred_element_type=F32)
    x1 = x_ref[...] + g1_ref[...] * mix
    h = _rms(x1) * n2_ref[...]
    h = (h * (1.0 + sc2_ref[...]) + sh2_ref[...]).astype(BF16)
    ff = jnp.zeros(x1.shape, F32)
    for c in range(D_FF // FF_CHUNK):
        cs = slice(c * FF_CHUNK, (c + 1) * FF_CHUNK)
        gate = jnp.dot(h, wg_ref[:, cs], preferred_element_type=F32)
        up = jnp.dot(h, wu_ref[:, cs], preferred_element_type=F32)
        ff = ff + jnp.dot((_silu(gate) * up).astype(BF16), wd_ref[cs, :], preferred_element_type=F32)
    x2 = x1 + g2_ref[...] * ff
    y_ref[...] = _rms(x2) * fg_ref[...]


def _ffn(x2, ret_y, att_y, mod_spec, mod_arr, n2, fg, wo, wg, wu, wd, tm):
    t, d = x2.shape
    row = lambda w: pl.BlockSpec((tm, w), lambda i: (i, 0))
    return pl.pallas_call(
        _ffn_kernel,
        grid=(t // tm,),
        in_specs=[row(d), row(RET_V_WIDTH), row(ATT_WIDTH), mod_spec(2), mod_spec(3), mod_spec(4), mod_spec(5),
                  _const_spec((1, d)), _const_spec((1, d)),
                  _const_spec(wo.shape), _const_spec(wg.shape), _const_spec(wu.shape), _const_spec(wd.shape)],
        out_specs=row(d),
        out_shape=jax.ShapeDtypeStruct((t, d), F32),
        compiler_params=_cparams("arbitrary"),
        name="outproj_ffn",
    )(x2, ret_y, att_y, mod_arr, mod_arr, mod_arr, mod_arr, n2, fg, wo, wg, wu, wd)


def _rope_tables(pos, inv_freq):
    ang = pos.astype(F32)[:, None] * inv_freq[None, :]
    cos, sin = jnp.cos(ang), jnp.sin(ang)
    reps = LANES // HEAD_DIM
    return (jnp.tile(jnp.concatenate([cos, cos], axis=1), (1, reps)),
            jnp.tile(jnp.concatenate([-sin, sin], axis=1), (1, reps)))


def _all_rope_tables(pos):
    half = jnp.arange(0, HEAD_DIM, 2, dtype=F32)
    ret_freq = ROPE_THETA ** (-jnp.linspace(0.0, 1.0, RET_QK_DIM // 2, dtype=F32))
    att_freq = ROPE_THETA ** (-half / HEAD_DIM)
    return _rope_tables(pos, ret_freq) + _rope_tables(pos, att_freq)


def kernel(x_prompt, x_sample, cache_attn_k, cache_attn_v, state_ret, c_prompt, c_sample,
           w_ada, b_ada, norm1_g, w_in, w_out, norm2_g, w_gate, w_up, w_down, final_g):
    b, s, d = x_prompt.shape
    bs, ts, _ = x_sample.shape
    depth = w_in.shape[0]
    assert depth == 1 and s % (BLK * max(DILATIONS)) == 0 and bs % SAMPLE_GROUP == 0
    buf = cache_attn_k.shape[2]

    xp = x_prompt.reshape(b * s, d)
    xs = x_sample.reshape(bs * ts, d)
    fg = final_g.reshape(1, d)
    kp_l, vp_l, rp_l, ks_l, vs_l, rs_l = [], [], [], [], [], []
    for l in range(depth):
        mod = _ada(jnp.concatenate([c_prompt, c_sample], axis=0), w_ada[l], b_ada[l])
        mod_p = mod[:b].reshape(b, 1, N_MOD * d)
        mod_s = jnp.repeat(mod[b:], ts, axis=0)
        w_in_b, wo_b = w_in[l].astype(BF16), w_out[l].astype(BF16)
        wg_b, wu_b, wd_b = w_gate[l].astype(BF16), w_up[l].astype(BF16), w_down[l].astype(BF16)
        g1, n2 = norm1_g[l].reshape(1, d), norm2_g[l].reshape(1, d)

        tm = 512
        per_seq = s // tm
        spec_p = lambda k: pl.BlockSpec((None, 1, d), lambda i, k=k: (i // per_seq, 0, k))
        tabs_p = _all_rope_tables(jnp.arange(s))
        rq, rk, rv, rg, aq, akb, avb, ak, av = _inproj(
            xp, spec_p, mod_p, g1, w_in_b, tabs_p, tm, lambda i: (i % per_seq, 0))
        ret_y, ret_state = _ret_prompt(rq, rk, rv, rg, b, s)
        att_y = _attn_prompt(aq, akb, avb, b, s)
        xp = _ffn(xp, ret_y, att_y, spec_p, mod_p, n2, fg, wo_b, wg_b, wu_b, wd_b, tm)
        win = min(2048, s)
        kp_l.append(ak.reshape(b, s, N_ATT_HEADS, HEAD_DIM)[:, s - win:])
        vp_l.append(av.reshape(b, s, N_ATT_HEADS, HEAD_DIM)[:, s - win:])
        rp_l.append(ret_state.reshape(b, N_RET_HEADS, RET_QK_DIM, RET_V_DIM))

        n_tok = bs * ts
        tms = min(512, n_tok)
        spec_s = lambda k: pl.BlockSpec((tms, d), lambda i, k=k: (i, k))
        tabs_s = tuple(jnp.tile(tb, (n_tok // ts, 1)) for tb in _all_rope_tables(PAST_LEN + jnp.arange(ts)))
        rq, rk, rv, rg, aq, akb, avb, ak, av = _inproj(
            xs, spec_s, mod_s, g1, w_in_b, tabs_s, tms, lambda i: (i, 0))
        state2 = state_ret[l].reshape(bs, N_RET_HEADS // 2, LANES, RET_V_DIM)
        ret_y, ret_state = _ret_sample(rq, rk, rv, rg, state2, ts)
        att_y = _attn_sample(aq.astype(F32).reshape(bs, ts, ATT_WIDTH), ak.reshape(bs, ts, ATT_WIDTH),
                             av.reshape(bs, ts, ATT_WIDTH), cache_attn_k[l].reshape(bs, buf, ATT_WIDTH),
                             cache_attn_v[l].reshape(bs, buf, ATT_WIDTH), ts)
        xs = _ffn(xs, ret_y, att_y.reshape(n_tok, ATT_WIDTH).astype(BF16), spec_s, mod_s, n2, fg,
                  wo_b, wg_b, wu_b, wd_b, tms)
        ks_l.append(ak.reshape(bs, ts, N_ATT_HEADS, HEAD_DIM))
        vs_l.append(av.reshape(bs, ts, N_ATT_HEADS, HEAD_DIM))
        rs_l.append(ret_state.reshape(bs, N_RET_HEADS, RET_QK_DIM, RET_V_DIM))

    return (xp.reshape(b, s, d), xs.reshape(bs, ts, d),
            jnp.stack(kp_l, 0), jnp.stack(vp_l, 0), jnp.stack(rp_l, 0),
            jnp.stack(ks_l, 0), jnp.stack(vs_l, 0), jnp.stack(rs_l, 0))
```

```python
import functools

import jax
import jax.numpy as jnp
import numpy as np
from jax import lax
from jax.experimental import pallas as pl
from jax.experimental.pallas import tpu as pltpu

D_MODEL = 1024
HEAD_DIM = 64
N_ATT_HEADS = 8
ATT_WIDTH = N_ATT_HEADS * HEAD_DIM
ATT_SCALE = HEAD_DIM ** -0.5
DILATIONS = (16, 4, 1)
N_BACK = 128
ROPE_THETA = 10000.0
N_RET_HEADS = 4
RET_QK_DIM = 64
RET_V_DIM = 128
RET_QK_WIDTH = N_RET_HEADS * RET_QK_DIM
RET_V_WIDTH = N_RET_HEADS * RET_V_DIM
RET_CHUNK = 128
D_FF = 2816
N_MOD = 6
EPS = 1e-6
PAST_LEN = 8192

LANES = 128
BLK = 128
NEG = -1e30
VMEM_LIMIT = 56 * 1024 * 1024
BF16 = jnp.bfloat16
F32 = jnp.float32

_NT = (((1,), (1,)), ((), ()))
_TN = (((0,), (0,)), ((), ()))


def _cparams(*sem):
    return pltpu.CompilerParams(dimension_semantics=sem, vmem_limit_bytes=VMEM_LIMIT)


def _const_spec(shape):
    n = len(shape)
    return pl.BlockSpec(shape, lambda *_: (0,) * n, pipeline_mode=pl.Buffered(1))


def _head_mask(hh, rows, dtype):
    lane = lax.broadcasted_iota(jnp.int32, (rows, LANES), 1)
    own = (lane >= HEAD_DIM) if hh else (lane < HEAD_DIM)
    return jnp.where(own, 1.0, 0.0).astype(dtype)


def _rms(x):
    return x * lax.rsqrt(jnp.mean(x * x, axis=-1, keepdims=True) + EPS)


def _silu(x):
    return x * jax.nn.sigmoid(x)


def _ada_kernel(c_ref, w_ref, b_ref, o_ref):
    c = _silu(c_ref[...]).astype(BF16)
    o_ref[...] = jnp.dot(c, w_ref[...].astype(BF16), preferred_element_type=F32) + b_ref[...]


def _ada(c_all, w_ada, b_ada):
    n, d = c_all.shape
    width = w_ada.shape[1]
    tn = 1536
    return pl.pallas_call(
        _ada_kernel,
        grid=(width // tn,),
        in_specs=[pl.BlockSpec((n, d), lambda j: (0, 0)),
                  pl.BlockSpec((d, tn), lambda j: (0, j)),
                  pl.BlockSpec((1, tn), lambda j: (0, j))],
        out_specs=pl.BlockSpec((n, tn), lambda j: (0, j)),
        out_shape=jax.ShapeDtypeStruct((n, width), F32),
        compiler_params=_cparams("arbitrary"),
        name="ada_mod",
    )(c_all, w_ada, b_ada.reshape(1, width))


def _rotate(z, cos, sin_signed):
    lane = lax.broadcasted_iota(jnp.int32, (z.shape[0], LANES), 1)
    first_half = (lane & (HEAD_DIM - 1)) < (HEAD_DIM // 2)
    outs = []
    for c in range(z.shape[1] // LANES):
        zc = z[:, c * LANES:(c + 1) * LANES]
        partner = jnp.where(first_half, pltpu.roll(zc, LANES - HEAD_DIM // 2, 1), pltpu.roll(zc, HEAD_DIM // 2, 1))
        outs.append(zc * cos + partner * sin_signed)
    return jnp.concatenate(outs, axis=1)


_IN_OFFS = tuple(int(o) for o in np.cumsum((0, RET_QK_WIDTH, RET_QK_WIDTH, RET_V_WIDTH, RET_V_WIDTH,
                                            ATT_WIDTH, ATT_WIDTH, ATT_WIDTH)))


STRIDED = tuple(d for d in DILATIONS if d > 1)
N_PAIRS = N_ATT_HEADS // 2


def _inproj_kernel(*refs, prompt):
    x_ref, sh_ref, sc_ref, g_ref, w_ref, cr_ref, sr_ref, ca_ref, sa_ref = refs[:9]
    rq_ref, rk_ref, rv_ref, rg_ref = refs[9:13]
    ak_ref, av_ref = refs[13:15]
    tm = x_ref.shape[0]
    h = _rms(x_ref[...]) * g_ref[...]
    h = (h * (1.0 + sc_ref[...]) + sh_ref[...]).astype(BF16)

    def proj(i):
        return jnp.dot(h, w_ref[:, _IN_OFFS[i]:_IN_OFFS[i + 1]], preferred_element_type=F32)

    cr, sr, ca, sa = cr_ref[...], sr_ref[...], ca_ref[...], sa_ref[...]
    rq_ref[...] = _rotate(proj(0), cr, sr).astype(BF16)
    rk_ref[...] = (_rotate(proj(1), cr, sr) * (RET_QK_DIM ** -0.5)).astype(BF16)
    rv_ref[...] = proj(2).astype(BF16)
    rg_ref[...] = proj(3).astype(BF16)
    aq = _rotate(proj(4), ca, sa) * ATT_SCALE
    ak = _rotate(proj(5), ca, sa)
    av = proj(6)
    ak_ref[...] = ak.reshape(tm, N_ATT_HEADS, HEAD_DIM)
    av_ref[...] = av.reshape(tm, N_ATT_HEADS, HEAD_DIM)
    if not prompt:
        refs[15][...] = aq.reshape(tm, N_ATT_HEADS, HEAD_DIM)
        return
    nat_refs = refs[15:18]
    strided_refs = refs[18:18 + 3 * len(STRIDED)]
    slabs = refs[18 + 3 * len(STRIDED)]
    for a, z in enumerate((aq, ak, av)):
        nat_refs[a][...] = z.astype(BF16)
        for p in range(N_PAIRS):
            slabs[a, p] = z[:, p * LANES:(p + 1) * LANES]
        for di, d in enumerate(STRIDED):
            o_ref = strided_refs[3 * di + a]
            for r in range(d):
                for p in range(N_PAIRS):
                    o_ref[r, :, p * LANES:(p + 1) * LANES] = slabs[a, p, pl.ds(r, tm // d, stride=d), :].astype(BF16)


def _inproj(x2, mod_spec, mod_arr, g1, w_in_b, tabs, tm, tab_map, seq=None):
    t, d = x2.shape
    prompt = seq is not None
    row = lambda w: pl.BlockSpec((tm, w), lambda i: (i, 0))
    heads = pl.BlockSpec((tm, N_ATT_HEADS, HEAD_DIM), lambda i: (i, 0, 0))
    tab = pl.BlockSpec((tm, LANES), tab_map)
    heads_shape = jax.ShapeDtypeStruct((t, N_ATT_HEADS, HEAD_DIM), F32)
    out_specs = [row(RET_QK_WIDTH), row(RET_QK_WIDTH), row(RET_V_WIDTH), row(RET_V_WIDTH), heads, heads]
    out_shape = [jax.ShapeDtypeStruct((t, w), BF16) for w in (RET_QK_WIDTH, RET_QK_WIDTH, RET_V_WIDTH, RET_V_WIDTH)]
    out_shape += [heads_shape, heads_shape]
    scratch = []
    if prompt:
        per_seq = seq // tm
        out_specs += [row(ATT_WIDTH)] * 3
        out_shape += [jax.ShapeDtypeStruct((t, ATT_WIDTH), BF16)] * 3
        for dd in STRIDED:
            spec = pl.BlockSpec((None, dd, tm // dd, ATT_WIDTH), lambda i: (i // per_seq, 0, i % per_seq, 0))
            out_specs += [spec] * 3
            out_shape += [jax.ShapeDtypeStruct((t // seq, dd, seq // dd, ATT_WIDTH), BF16)] * 3
        scratch = [pltpu.VMEM((3, N_PAIRS, tm, LANES), F32)]
    else:
        out_specs += [heads]
        out_shape += [heads_shape]
    return pl.pallas_call(
        functools.partial(_inproj_kernel, prompt=prompt),
        grid=(t // tm,),
        in_specs=[row(d), mod_spec(0), mod_spec(1), _const_spec((1, d)), _const_spec(w_in_b.shape),
                  tab, tab, tab, tab],
        out_specs=out_specs,
        out_shape=out_shape,
        scratch_shapes=scratch,
        compiler_params=_cparams("arbitrary"),
        name="inproj_rope",
    )(x2, mod_arr, mod_arr, g1, w_in_b, *tabs)


def _ret_prompt_kernel(q_ref, k_ref, v_ref, g_ref, intra_ref, xi_ref, zeta_ref, gdec_ref,
                       y_ref, st_ref, r_ref):
    n_chunks = q_ref.shape[0] // RET_CHUNK
    r_ref[...] = jnp.zeros_like(r_ref)

    def chunk(c, carry):
        rows = pl.ds(pl.multiple_of(c * RET_CHUNK, RET_CHUNK), RET_CHUNK)
        for p in range(N_RET_HEADS // 2):
            pc = slice(p * LANES, (p + 1) * LANES)
            qp = q_ref[rows, pc]
            kp = k_ref[rows, pc]
            r_old = r_ref[p]
            r_new = r_old * gdec_ref[p]
            r_b = r_old.astype(BF16)
            for hh in range(2):
                h = 2 * p + hh
                hc = slice(h * RET_V_DIM, (h + 1) * RET_V_DIM)
                qm = qp * _head_mask(hh, RET_CHUNK, BF16)
                s = lax.dot_general(qm, kp, _NT, preferred_element_type=F32) * intra_ref[h]
                vh = v_ref[rows, hc]
                o = jnp.dot(s.astype(BF16), vh, preferred_element_type=F32)
                o = o + jnp.dot(qm, r_b, preferred_element_type=F32) * xi_ref[h]
                y_ref[rows, hc] = (_rms(o) * _silu(g_ref[rows, hc].astype(F32))).astype(BF16)
                kz = (kp.astype(F32) * zeta_ref[h] * _head_mask(hh, RET_CHUNK, F32)).astype(BF16)
                r_new = r_new + lax.dot_general(kz, vh, _TN, preferred_element_type=F32)
            r_ref[p] = r_new
        return carry

    lax.fori_loop(0, n_chunks, chunk, 0)
    st_ref[...] = r_ref[...]


def _ret_tables(chunk, reps):
    h = N_RET_HEADS
    log_g = jnp.log1p(-(2.0 ** (-5.0 - jnp.arange(h, dtype=F32))))
    idx = jnp.arange(chunk, dtype=F32)
    diff = idx[:, None] - idx[None, :]
    intra = jnp.where(diff >= 0, jnp.exp(jnp.maximum(diff, 0.0) * log_g[:, None, None]), 0.0)
    xi = jnp.exp((idx + 1.0)[None, :] * log_g[:, None])
    zeta = jnp.exp((chunk - 1.0 - idx)[None, :] * log_g[:, None])
    g_chunk = jnp.exp(chunk * log_g)
    eye = jnp.eye(reps, dtype=F32)
    intra_bd = jnp.einsum("ab,hij->haibj", eye, intra).reshape(h, reps * chunk, reps * chunk)
    n = reps * chunk
    xi_t = jnp.broadcast_to(jnp.tile(xi, (1, reps))[:, :, None], (h, n, RET_V_DIM))
    zeta_t = jnp.broadcast_to(jnp.tile(zeta, (1, reps))[:, :, None], (h, n, LANES))
    gdec = jnp.broadcast_to(jnp.repeat(g_chunk, RET_QK_DIM).reshape(h // 2, LANES, 1), (h // 2, LANES, RET_V_DIM))
    return intra_bd, xi_t, zeta_t, gdec


def _ret_prompt(rq, rk, rv, rg, b, s):
    tabs = _ret_tables(RET_CHUNK, 1)
    seq = lambda w: pl.BlockSpec((s, w), lambda i: (i, 0))
    return pl.pallas_call(
        _ret_prompt_kernel,
        grid=(b,),
        in_specs=[seq(RET_QK_WIDTH), seq(RET_QK_WIDTH), seq(RET_V_WIDTH), seq(RET_V_WIDTH)]
                 + [_const_spec(t.shape) for t in tabs],
        out_specs=[seq(RET_V_WIDTH),
                   pl.BlockSpec((None, N_RET_HEADS // 2, LANES, RET_V_DIM), lambda i: (i, 0, 0, 0))],
        out_shape=[jax.ShapeDtypeStruct((b * s, RET_V_WIDTH), BF16),
                   jax.ShapeDtypeStruct((b, N_RET_HEADS // 2, LANES, RET_V_DIM), F32)],
        scratch_shapes=[pltpu.VMEM((N_RET_HEADS // 2, LANES, RET_V_DIM), F32)],
        compiler_params=_cparams("arbitrary"),
        name="ret_prompt",
    )(rq, rk, rv, rg, *tabs)


SAMPLE_GROUP = 32


def _ret_sample_kernel(q_ref, k_ref, v_ref, g_ref, st_ref, intra_ref, xi_ref, zeta_ref, gdec_ref,
                       y_ref, so_ref, *, dec_seq):
    n = q_ref.shape[0]
    nb = n // dec_seq
    tok_seq = lax.broadcasted_iota(jnp.int32, (n, LANES), 0) // dec_seq
    seq_of_col = lax.broadcasted_iota(jnp.int32, (LANES, n), 1) // dec_seq
    for p in range(N_RET_HEADS // 2):
        pc = slice(p * LANES, (p + 1) * LANES)
        qp = q_ref[:, pc]
        kp = k_ref[:, pc]
        st = st_ref[:, p]
        st_flat = st.reshape(nb * LANES, RET_V_DIM).astype(BF16)
        upd = jnp.zeros((nb * LANES, RET_V_DIM), F32)
        for hh in range(2):
            h = 2 * p + hh
            hc = slice(h * RET_V_DIM, (h + 1) * RET_V_DIM)
            qm = qp * _head_mask(hh, n, BF16)
            s = lax.dot_general(qm, kp, _NT, preferred_element_type=F32) * intra_ref[h]
            vh = v_ref[:, hc]
            o = jnp.dot(s.astype(BF16), vh, preferred_element_type=F32)
            qf = qm.astype(F32)
            qexp = jnp.concatenate([jnp.where(tok_seq == b, qf, 0.0) for b in range(nb)], axis=1).astype(BF16)
            o = o + jnp.dot(qexp, st_flat, preferred_element_type=F32) * xi_ref[h]
            y_ref[:, hc] = (_rms(o) * _silu(g_ref[:, hc].astype(F32))).astype(BF16)
            kzt = (kp.astype(F32) * zeta_ref[h] * _head_mask(hh, n, F32)).T
            w = jnp.concatenate([jnp.where(seq_of_col == b, kzt, 0.0) for b in range(nb)], axis=0).astype(BF16)
            upd = upd + jnp.dot(w, vh, preferred_element_type=F32)
        so_ref[:, p] = st * gdec_ref[p] + upd.reshape(nb, LANES, RET_V_DIM)


def _ret_sample(rq, rk, rv, rg, state, dec_seq):
    b = state.shape[0]
    grp = SAMPLE_GROUP
    n = grp * dec_seq
    tabs = _ret_tables(dec_seq, grp)
    tok = lambda w: pl.BlockSpec((n, w), lambda i: (i, 0))
    st_spec = pl.BlockSpec((grp, N_RET_HEADS // 2, LANES, RET_V_DIM), lambda i: (i, 0, 0, 0))
    return pl.pallas_call(
        functools.partial(_ret_sample_kernel, dec_seq=dec_seq),
        grid=(b // grp,),
        in_specs=[tok(RET_QK_WIDTH), tok(RET_QK_WIDTH), tok(RET_V_WIDTH), tok(RET_V_WIDTH), st_spec]
                 + [_const_spec(t.shape) for t in tabs],
        out_specs=[tok(RET_V_WIDTH), st_spec],
        out_shape=[jax.ShapeDtypeStruct((b * dec_seq, RET_V_WIDTH), BF16),
                   jax.ShapeDtypeStruct(state.shape, F32)],
        compiler_params=_cparams("arbitrary"),
        name="ret_sample",
    )(rq, rk, rv, rg, state, *tabs)


def _attn_prompt_kernel(*refs, seq):
    n_pat = len(DILATIONS)
    in_refs = refs[:5 * n_pat]
    o_ref = refs[5 * n_pat]
    acc, m_s, l_s = refs[5 * n_pat + 1:]
    t = pl.program_id(1)
    n_pairs = N_ATT_HEADS // 2

    @pl.when(t == 0)
    def _():
        m_s[...] = jnp.full(m_s.shape, NEG, F32)
        l_s[...] = jnp.zeros(l_s.shape, F32)
        acc[...] = jnp.zeros(acc.shape, F32)

    row = lax.broadcasted_iota(jnp.int32, (BLK, 2 * BLK), 0)
    col = lax.broadcasted_iota(jnp.int32, (BLK, 2 * BLK), 1)
    band = (col >= row) & (col <= row + N_BACK)
    lo = lax.broadcasted_iota(jnp.int32, (BLK, LANES), 1) < HEAD_DIM

    for pi, d in enumerate(DILATIONS):
        q_ref, kp_ref, kc_ref, vp_ref, vc_ref = in_refs[5 * pi:5 * pi + 5]
        nb = seq // d // BLK
        i = t % nb
        r = t // nb
        valid = band & (col >= jnp.where(i > 0, 0, BLK))
        if d == 1:
            nat = pl.ds(pl.multiple_of(t * BLK, BLK), BLK)
        else:
            nat = pl.ds(r + d * BLK * i, BLK, stride=d)
        for p in range(n_pairs):
            pc = slice(p * LANES, (p + 1) * LANES)
            qp = q_ref[:, pc]
            kw = jnp.concatenate([kp_ref[:, pc], kc_ref[:, pc]], axis=0)
            vw = jnp.concatenate([vp_ref[:, pc], vc_ref[:, pc]], axis=0)
            stats = []
            for hh in range(2):
                s = lax.dot_general(qp * _head_mask(hh, BLK, BF16), kw, _NT, preferred_element_type=F32)
                s = jnp.where(valid, s, NEG)
                m = jnp.max(s, axis=-1, keepdims=True)
                e = jnp.exp(s - m)
                l = jnp.sum(e, axis=-1, keepdims=True)
                o = jnp.dot(e.astype(BF16), vw, preferred_element_type=F32)
                stats.append((m, l, o))
            m_p = jnp.where(lo, stats[0][0], stats[1][0])
            l_p = jnp.where(lo, stats[0][1], stats[1][1])
            o_p = jnp.where(lo, stats[0][2], stats[1][2])
            m_o = m_s[p, nat, :]
            m_n = jnp.maximum(m_o, m_p)
            a_o = jnp.exp(m_o - m_n)
            a_p = jnp.exp(m_p - m_n)
            m_s[p, nat, :] = m_n
            l_s[p, nat, :] = a_o * l_s[p, nat, :] + a_p * l_p
            acc[p, nat, :] = a_o * acc[p, nat, :] + a_p * o_p

    @pl.when(t == pl.num_programs(1) - 1)
    def _():
        for p in range(n_pairs):
            o_ref[:, p * LANES:(p + 1) * LANES] = (acc[p] / l_s[p]).astype(BF16)


def _attn_prompt(qkv_by_dilation, b, s):
    n_blk = s // BLK
    args, specs = [], []
    for d in DILATIONS:
        nb = s // d // BLK
        q, k, v = qkv_by_dilation[d]
        if d == 1:
            q, k, v = (a.reshape(b, 1, s, ATT_WIDTH) for a in (q, k, v))
        shape = (None, None, BLK, ATT_WIDTH)
        cur = lambda bi, t, nb=nb: (bi, t // nb, t % nb, 0)
        prev = lambda bi, t, nb=nb: (bi, t // nb, jnp.maximum(t % nb - 1, 0), 0)
        args += [q, k, k, v, v]
        specs += [pl.BlockSpec(shape, cur), pl.BlockSpec(shape, prev), pl.BlockSpec(shape, cur),
                  pl.BlockSpec(shape, prev), pl.BlockSpec(shape, cur)]
    slab = pltpu.VMEM((N_ATT_HEADS // 2, s, LANES), F32)
    return pl.pallas_call(
        functools.partial(_attn_prompt_kernel, seq=s),
        grid=(b, n_blk),
        in_specs=specs,
        out_specs=pl.BlockSpec((s, ATT_WIDTH), lambda bi, t: (bi, 0)),
        out_shape=jax.ShapeDtypeStruct((b * s, ATT_WIDTH), BF16),
        scratch_shapes=[slab, slab, slab],
        compiler_params=_cparams("arbitrary", "arbitrary"),
        name="attn_prompt",
    )(*args)


def _attn_sample_kernel(q_ref, kn_ref, vn_ref, kf_ref, vf_ref, kt_ref, vt_ref, pf_ref, pt_ref, pn_ref, o_ref,
                        *, dec_seq, buf):
    n_rows = dec_seq * N_ATT_HEADS
    q2 = q_ref[...].reshape(n_rows, HEAD_DIM)
    row = lax.broadcasted_iota(jnp.int32, (n_rows, 1), 0)
    t_row, h_row = row >> (N_ATT_HEADS.bit_length() - 1), row & (N_ATT_HEADS - 1)

    def segment(k_ref, v_ref, p_ref):
        n = p_ref.shape[1]
        k2 = k_ref[...].reshape(n, HEAD_DIM)
        v2 = v_ref[...].reshape(n, HEAD_DIM)
        s = lax.dot_general(q2, k2, _NT, preferred_element_type=F32)
        dist = (buf + t_row) - p_ref[0:1, :]
        cnt = jnp.zeros((n_rows, n), F32)
        for d in DILATIONS:
            hit = (dist >= 0) & (dist <= d * N_BACK) & ((dist & (d - 1)) == 0)
            cnt = cnt + jnp.where(hit, 1.0, 0.0)
        cnt = jnp.where(p_ref[1:2, :] == h_row, cnt, 0.0)
        s = jnp.where(cnt > 0.0, s, NEG)
        m = jnp.max(s, axis=-1, keepdims=True)
        e = cnt * jnp.exp(s - m)
        return m, jnp.sum(e, axis=-1, keepdims=True), jnp.dot(e, v2, preferred_element_type=F32)

    parts = [segment(kf_ref, vf_ref, pf_ref), segment(kt_ref, vt_ref, pt_ref), segment(kn_ref, vn_ref, pn_ref)]
    m = functools.reduce(jnp.maximum, [p[0] for p in parts])
    l = sum(jnp.exp(p[0] - m) * p[1] for p in parts)
    o = sum(jnp.exp(p[0] - m) * p[2] for p in parts)
    o_ref[...] = (o / l).reshape(dec_seq, N_ATT_HEADS, HEAD_DIM)


def _attn_sample(aq, ak, av, cache_k, cache_v):
    b, ts = aq.shape[:2]
    buf = cache_k.shape[1]
    d_max = max(DILATIONS)
    tail = max(d for d in DILATIONS if d < d_max) * N_BACK
    assert buf == d_max * N_BACK and ts <= d_max and (buf - tail) % d_max == 0
    n_far = (buf - tail) // d_max
    hd = (N_ATT_HEADS, HEAD_DIM)
    far = lambda c: c.reshape(b, buf // d_max, d_max, *hd)
    tl = lambda c: c.reshape(b, buf // tail, tail, *hd)
    tok = pl.BlockSpec((None, ts, *hd), lambda i: (i, 0, 0, 0))
    far_spec = pl.BlockSpec((None, n_far, ts, *hd), lambda i: (i, 0, 0, 0, 0))
    tail_spec = pl.BlockSpec((None, None, tail, *hd), lambda i: (i, buf // tail - 1, 0, 0, 0))

    def rows(pos):
        pos = np.asarray(pos, np.int32)
        return jnp.asarray(np.stack([np.repeat(pos, N_ATT_HEADS), np.tile(np.arange(N_ATT_HEADS, dtype=np.int32), pos.size)]))

    p_far = rows((np.arange(n_far)[:, None] * d_max + np.arange(ts)[None, :]).reshape(-1))
    p_tail = rows(buf - tail + np.arange(tail))
    p_new = rows(buf + np.arange(ts))
    return pl.pallas_call(
        functools.partial(_attn_sample_kernel, dec_seq=ts, buf=buf),
        grid=(b,),
        in_specs=[tok, tok, tok, far_spec, far_spec, tail_spec, tail_spec,
                  _const_spec(p_far.shape), _const_spec(p_tail.shape), _const_spec(p_new.shape)],
        out_specs=tok,
        out_shape=jax.ShapeDtypeStruct((b, ts, *hd), F32),
        compiler_params=_cparams("arbitrary"),
        name="attn_sample",
    )(aq, ak, av, far(cache_k), far(cache_v), tl(cache_k), tl(cache_v), p_far, p_tail, p_new)


FF_CHUNK = 256


def _ffn_kernel(x_ref, ry_ref, ay_ref, g1_ref, sh2_ref, sc2_ref, g2_ref, n2_ref, fg_ref,
                wo_ref, wg_ref, wu_ref, wd_ref, y_ref):
    mix = jnp.dot(ry_ref[...], wo_ref[0:RET_V_WIDTH, :], preferred_element_type=F32)
    mix = mix + jnp.dot(ay_ref[...], wo_ref[RET_V_WIDTH:, :], preferred_element_type=F32)
    x1 = x_ref[...] + g1_ref[...] * mix
    h = _rms(x1) * n2_ref[...]
    h = (h * (1.0 + sc2_ref[...]) + sh2_ref[...]).astype(BF16)
    ff = jnp.zeros(x1.shape, F32)
    for c in range(D_FF // FF_CHUNK):
        cs = slice(c * FF_CHUNK, (c + 1) * FF_CHUNK)
        gate = jnp.dot(h, wg_ref[:, cs], preferred_element_type=F32)
        up = jnp.dot(h, wu_ref[:, cs], preferred_element_type=F32)
        ff = ff + jnp.dot((_silu(gate) * up).astype(BF16), wd_ref[cs, :], preferred_element_type=F32)
    x2 = x1 + g2_ref[...] * ff
    y_ref[...] = _rms(x2) * fg_ref[...]


def _ffn(x2, ret_y, att_y, mod_spec, mod_arr, n2, fg, wo, wg, wu, wd, tm):
    t, d = x2.shape
    row = lambda w: pl.BlockSpec((tm, w), lambda i: (i, 0))
    return pl.pallas_call(
        _ffn_kernel,
        grid=(t // tm,),
        in_specs=[row(d), row(RET_V_WIDTH), row(ATT_WIDTH), mod_spec(2), mod_spec(3), mod_spec(4), mod_spec(5),
                  _const_spec((1, d)), _const_spec((1, d)),
                  _const_spec(wo.shape), _const_spec(wg.shape), _const_spec(wu.shape), _const_spec(wd.shape)],
        out_specs=row(d),
        out_shape=jax.ShapeDtypeStruct((t, d), F32),
        compiler_params=_cparams("arbitrary"),
        name="outproj_ffn",
    )(x2, ret_y, att_y, mod_arr, mod_arr, mod_arr, mod_arr, n2, fg, wo, wg, wu, wd)


def _rope_tables(pos, inv_freq):
    ang = pos.astype(F32)[:, None] * inv_freq[None, :]
    cos, sin = jnp.cos(ang), jnp.sin(ang)
    reps = LANES // HEAD_DIM
    return (jnp.tile(jnp.concatenate([cos, cos], axis=1), (1, reps)),
            jnp.tile(jnp.concatenate([-sin, sin], axis=1), (1, reps)))


def _all_rope_tables(pos):
    half = jnp.arange(0, HEAD_DIM, 2, dtype=F32)
    ret_freq = ROPE_THETA ** (-jnp.linspace(0.0, 1.0, RET_QK_DIM // 2, dtype=F32))
    att_freq = ROPE_THETA ** (-half / HEAD_DIM)
    return _rope_tables(pos, ret_freq) + _rope_tables(pos, att_freq)


def kernel(x_prompt, x_sample, cache_attn_k, cache_attn_v, state_ret, c_prompt, c_sample,
           w_ada, b_ada, norm1_g, w_in, w_out, norm2_g, w_gate, w_up, w_down, final_g):
    b, s, d = x_prompt.shape
    bs, ts, _ = x_sample.shape
    depth = w_in.shape[0]
    assert depth == 1 and s % (BLK * max(DILATIONS)) == 0 and bs % SAMPLE_GROUP == 0
    buf = cache_attn_k.shape[2]

    xp = x_prompt.reshape(b * s, d)
    xs = x_sample.reshape(bs * ts, d)
    fg = final_g.reshape(1, d)
    kp_l, vp_l, rp_l, ks_l, vs_l, rs_l = [], [], [], [], [], []
    for l in range(depth):
        mod = _ada(jnp.concatenate([c_prompt, c_sample], axis=0), w_ada[l], b_ada[l])
        mod_p = mod[:b].reshape(b, 1, N_MOD * d)
        mod_s = jnp.repeat(mod[b:], ts, axis=0)
        w_in_b, wo_b = w_in[l].astype(BF16), w_out[l].astype(BF16)
        wg_b, wu_b, wd_b = w_gate[l].astype(BF16), w_up[l].astype(BF16), w_down[l].astype(BF16)
        g1, n2 = norm1_g[l].reshape(1, d), norm2_g[l].reshape(1, d)

        tm = 512
        per_seq = s // tm
        spec_p = lambda k: pl.BlockSpec((None, 1, d), lambda i, k=k: (i // per_seq, 0, k))
        tabs_p = _all_rope_tables(jnp.arange(s))
        outs = _inproj(xp, spec_p, mod_p, g1, w_in_b, tabs_p, tm, lambda i: (i % per_seq, 0), seq=s)
        rq, rk, rv, rg, ak, av = outs[:6]
        qkv = {1: outs[6:9]}
        for di, dd in enumerate(STRIDED):
            qkv[dd] = outs[9 + 3 * di:12 + 3 * di]
        ret_y, ret_state = _ret_prompt(rq, rk, rv, rg, b, s)
        att_y = _attn_prompt(qkv, b, s)
        xp = _ffn(xp, ret_y, att_y, spec_p, mod_p, n2, fg, wo_b, wg_b, wu_b, wd_b, tm)
        win = min(2048, s)
        kp_l.append(ak.reshape(b, s, N_ATT_HEADS, HEAD_DIM)[:, s - win:])
        vp_l.append(av.reshape(b, s, N_ATT_HEADS, HEAD_DIM)[:, s - win:])
        rp_l.append(ret_state.reshape(b, N_RET_HEADS, RET_QK_DIM, RET_V_DIM))

        n_tok = bs * ts
        tms = min(512, n_tok)
        spec_s = lambda k: pl.BlockSpec((tms, d), lambda i, k=k: (i, k))
        tabs_s = tuple(jnp.tile(tb, (n_tok // ts, 1)) for tb in _all_rope_tables(PAST_LEN + jnp.arange(ts)))
        rq, rk, rv, rg, ak, av, aq = _inproj(xs, spec_s, mod_s, g1, w_in_b, tabs_s, tms, lambda i: (i, 0))
        state2 = state_ret[l].reshape(bs, N_RET_HEADS // 2, LANES, RET_V_DIM)
        ret_y, ret_state = _ret_sample(rq, rk, rv, rg, state2, ts)
        by_seq = lambda a: a.reshape(bs, ts, N_ATT_HEADS, HEAD_DIM)
        att_y = _attn_sample(by_seq(aq), by_seq(ak), by_seq(av), cache_attn_k[l], cache_attn_v[l])
        xs = _ffn(xs, ret_y, att_y.reshape(n_tok, ATT_WIDTH).astype(BF16), spec_s, mod_s, n2, fg,
                  wo_b, wg_b, wu_b, wd_b, tms)
        ks_l.append(ak.reshape(bs, ts, N_ATT_HEADS, HEAD_DIM))
        vs_l.append(av.reshape(bs, ts, N_ATT_HEADS, HEAD_DIM))
        rs_l.append(ret_state.reshape(bs, N_RET_HEADS, RET_QK_DIM, RET_V_DIM))

    return (xp.reshape(b, s, d), xs.reshape(bs, ts, d),
            jnp.stack(kp_l, 0), jnp.stack(vp_l, 0), jnp.stack(rp_l, 0),
            jnp.stack(ks_l, 0), jnp.stack(vs_l, 0), jnp.stack(rs_l, 0))
```

```python
import functools

import jax
import jax.numpy as jnp
import numpy as np
from jax import lax
from jax.experimental import pallas as pl
from jax.experimental.pallas import tpu as pltpu

D_MODEL = 1024
HEAD_DIM = 64
N_ATT_HEADS = 8
ATT_WIDTH = N_ATT_HEADS * HEAD_DIM
ATT_SCALE = HEAD_DIM ** -0.5
LOG2E = 1.4426950408889634
DILATIONS = (16, 4, 1)
N_BACK = 128
ROPE_THETA = 10000.0
N_RET_HEADS = 4
RET_QK_DIM = 64
RET_V_DIM = 128
RET_QK_WIDTH = N_RET_HEADS * RET_QK_DIM
RET_V_WIDTH = N_RET_HEADS * RET_V_DIM
RET_CHUNK = 128
D_FF = 2816
N_MOD = 6
EPS = 1e-6
PAST_LEN = 8192

LANES = 128
BLK = 128
NEG = -1e30
VMEM_LIMIT = 56 * 1024 * 1024
BF16 = jnp.bfloat16
F32 = jnp.float32

_NT = (((1,), (1,)), ((), ()))
_TN = (((0,), (0,)), ((), ()))


def _cparams(*sem):
    return pltpu.CompilerParams(dimension_semantics=sem, vmem_limit_bytes=VMEM_LIMIT)


def _const_spec(shape):
    n = len(shape)
    return pl.BlockSpec(shape, lambda *_: (0,) * n, pipeline_mode=pl.Buffered(1))


def _head_mask(hh, rows, dtype):
    lane = lax.broadcasted_iota(jnp.int32, (rows, LANES), 1)
    own = (lane >= HEAD_DIM) if hh else (lane < HEAD_DIM)
    return jnp.where(own, 1.0, 0.0).astype(dtype)


def _rms(x):
    return x * lax.rsqrt(jnp.mean(x * x, axis=-1, keepdims=True) + EPS)


def _silu(x):
    return x * jax.nn.sigmoid(x)


def _ada_kernel(c_ref, w_ref, b_ref, o_ref):
    c = _silu(c_ref[...]).astype(BF16)
    o_ref[...] = jnp.dot(c, w_ref[...].astype(BF16), preferred_element_type=F32) + b_ref[...]


def _ada(c_all, w_ada, b_ada):
    n, d = c_all.shape
    width = w_ada.shape[1]
    tn = 1536
    return pl.pallas_call(
        _ada_kernel,
        grid=(width // tn,),
        in_specs=[pl.BlockSpec((n, d), lambda j: (0, 0)),
                  pl.BlockSpec((d, tn), lambda j: (0, j)),
                  pl.BlockSpec((1, tn), lambda j: (0, j))],
        out_specs=pl.BlockSpec((n, tn), lambda j: (0, j)),
        out_shape=jax.ShapeDtypeStruct((n, width), F32),
        compiler_params=_cparams("arbitrary"),
        name="ada_mod",
    )(c_all, w_ada, b_ada.reshape(1, width))


def _rotate(z, cos, sin_signed):
    lane = lax.broadcasted_iota(jnp.int32, (z.shape[0], LANES), 1)
    first_half = (lane & (HEAD_DIM - 1)) < (HEAD_DIM // 2)
    outs = []
    for c in range(z.shape[1] // LANES):
        zc = z[:, c * LANES:(c + 1) * LANES]
        partner = jnp.where(first_half, pltpu.roll(zc, LANES - HEAD_DIM // 2, 1), pltpu.roll(zc, HEAD_DIM // 2, 1))
        outs.append(zc * cos + partner * sin_signed)
    return jnp.concatenate(outs, axis=1)


_IN_OFFS = tuple(int(o) for o in np.cumsum((0, RET_QK_WIDTH, RET_QK_WIDTH, RET_V_WIDTH, RET_V_WIDTH,
                                            ATT_WIDTH, ATT_WIDTH, ATT_WIDTH)))


STRIDED = tuple(sorted(d for d in DILATIONS if d > 1))
STEP = 4
assert all(d == STEP ** (i + 1) for i, d in enumerate(STRIDED))
N_PAIRS = N_ATT_HEADS // 2


def _inproj_kernel(*refs, prompt):
    x_ref, sh_ref, sc_ref, g_ref, w_ref, cr_ref, sr_ref, ca_ref, sa_ref = refs[:9]
    rq_ref, rk_ref, rv_ref, rg_ref = refs[9:13]
    ak_ref, av_ref = refs[13:15]
    tm = x_ref.shape[0]
    h = _rms(x_ref[...]) * g_ref[...]
    h = (h * (1.0 + sc_ref[...]) + sh_ref[...]).astype(BF16)

    def proj(i):
        return jnp.dot(h, w_ref[:, _IN_OFFS[i]:_IN_OFFS[i + 1]], preferred_element_type=F32)

    cr, sr, ca, sa = cr_ref[...], sr_ref[...], ca_ref[...], sa_ref[...]
    rq_ref[...] = _rotate(proj(0), cr, sr).astype(BF16)
    rk_ref[...] = (_rotate(proj(1), cr, sr) * (RET_QK_DIM ** -0.5)).astype(BF16)
    rv_ref[...] = proj(2).astype(BF16)
    rg_ref[...] = proj(3).astype(BF16)
    aq = _rotate(proj(4), ca, sa) * (ATT_SCALE * LOG2E)
    ak = _rotate(proj(5), ca, sa)
    av = proj(6)
    if not prompt:
        for ref, z in zip((ak_ref, av_ref, refs[15]), (ak, av, aq)):
            ref[...] = z.reshape(tm, N_ATT_HEADS, HEAD_DIM)
        return
    ak_ref[...] = ak.T.reshape(N_ATT_HEADS, HEAD_DIM, tm)
    av_ref[...] = av.T.reshape(N_ATT_HEADS, HEAD_DIM, tm)
    nat_refs = refs[15:18]
    strided_refs = refs[18:18 + 3 * len(STRIDED)]
    slabs = refs[18 + 3 * len(STRIDED)]
    for a, z in enumerate((aq, ak, av)):
        nat_refs[a][...] = z.astype(BF16)
        for p in range(N_PAIRS):
            slabs[0, a, p] = z[:, p * LANES:(p + 1) * LANES]
        for lvl, d in enumerate(STRIDED):
            o_ref = strided_refs[3 * lvl + a]
            coarse = d // STEP
            n = tm // d
            for r_old in range(coarse):
                for r_new in range(STEP):
                    r = r_new * coarse + r_old
                    for p in range(N_PAIRS):
                        v = slabs[lvl, a, p, pl.ds(r_old * (tm // coarse) + r_new, n, stride=STEP), :]
                        o_ref[r, :, p * LANES:(p + 1) * LANES] = v.astype(BF16)
                        if lvl + 1 < len(STRIDED):
                            slabs[lvl + 1, a, p, r * n:(r + 1) * n, :] = v


def _inproj(x2, mod_spec, mod_arr, g1, w_in_b, tabs, tm, tab_map, seq=None):
    t, d = x2.shape
    prompt = seq is not None
    row = lambda w: pl.BlockSpec((tm, w), lambda i: (i, 0))
    heads = pl.BlockSpec((tm, N_ATT_HEADS, HEAD_DIM), lambda i: (i, 0, 0))
    tab = pl.BlockSpec((tm, LANES), tab_map)
    heads_shape = jax.ShapeDtypeStruct((t, N_ATT_HEADS, HEAD_DIM), F32)
    out_specs = [row(RET_QK_WIDTH), row(RET_QK_WIDTH), row(RET_V_WIDTH), row(RET_V_WIDTH)]
    out_shape = [jax.ShapeDtypeStruct((t, w), BF16) for w in (RET_QK_WIDTH, RET_QK_WIDTH, RET_V_WIDTH, RET_V_WIDTH)]
    scratch = []
    if prompt:
        per_seq = seq // tm
        by_seq = lambda i: (i // per_seq, 0, 0, i % per_seq)
        out_specs += [pl.BlockSpec((None, N_ATT_HEADS, HEAD_DIM, tm), by_seq)] * 2
        out_shape += [jax.ShapeDtypeStruct((t // seq, N_ATT_HEADS, HEAD_DIM, seq), F32)] * 2
        out_specs += [row(ATT_WIDTH)] * 3
        out_shape += [jax.ShapeDtypeStruct((t, ATT_WIDTH), BF16)] * 3
        for dd in STRIDED:
            spec = pl.BlockSpec((None, dd, tm // dd, ATT_WIDTH), lambda i: (i // per_seq, 0, i % per_seq, 0))
            out_specs += [spec] * 3
            out_shape += [jax.ShapeDtypeStruct((t // seq, dd, seq // dd, ATT_WIDTH), BF16)] * 3
        scratch = [pltpu.VMEM((len(STRIDED), 3, N_PAIRS, tm, LANES), F32)]
    else:
        out_specs += [heads] * 3
        out_shape += [heads_shape] * 3
    return pl.pallas_call(
        functools.partial(_inproj_kernel, prompt=prompt),
        grid=(t // tm,),
        in_specs=[row(d), mod_spec(0), mod_spec(1), _const_spec((1, d)), _const_spec(w_in_b.shape),
                  tab, tab, tab, tab],
        out_specs=out_specs,
        out_shape=out_shape,
        scratch_shapes=scratch,
        compiler_params=_cparams("arbitrary"),
        name="inproj_rope",
    )(x2, mod_arr, mod_arr, g1, w_in_b, *tabs)


def _ret_prompt_kernel(q_ref, k_ref, v_ref, g_ref, intra_ref, xi_ref, zeta_ref, gdec_ref,
                       y_ref, st_ref, r_ref):
    n_chunks = q_ref.shape[0] // RET_CHUNK
    r_ref[...] = jnp.zeros_like(r_ref)

    def chunk(c, carry):
        rows = pl.ds(pl.multiple_of(c * RET_CHUNK, RET_CHUNK), RET_CHUNK)
        for p in range(N_RET_HEADS // 2):
            pc = slice(p * LANES, (p + 1) * LANES)
            qp = q_ref[rows, pc]
            kp = k_ref[rows, pc]
            r_old = r_ref[p]
            r_new = r_old * gdec_ref[p]
            r_b = r_old.astype(BF16)
            for hh in range(2):
                h = 2 * p + hh
                hc = slice(h * RET_V_DIM, (h + 1) * RET_V_DIM)
                qm = qp * _head_mask(hh, RET_CHUNK, BF16)
                s = lax.dot_general(qm, kp, _NT, preferred_element_type=F32) * intra_ref[h]
                vh = v_ref[rows, hc]
                o = jnp.dot(s.astype(BF16), vh, preferred_element_type=F32)
                o = o + jnp.dot(qm, r_b, preferred_element_type=F32) * xi_ref[h]
                y_ref[rows, hc] = (_rms(o) * _silu(g_ref[rows, hc].astype(F32))).astype(BF16)
                kz = (kp.astype(F32) * zeta_ref[h] * _head_mask(hh, RET_CHUNK, F32)).astype(BF16)
                r_new = r_new + lax.dot_general(kz, vh, _TN, preferred_element_type=F32)
            r_ref[p] = r_new
        return carry

    lax.fori_loop(0, n_chunks, chunk, 0)
    st_ref[...] = r_ref[...]


def _ret_tables(chunk, reps):
    h = N_RET_HEADS
    log_g = jnp.log1p(-(2.0 ** (-5.0 - jnp.arange(h, dtype=F32))))
    idx = jnp.arange(chunk, dtype=F32)
    diff = idx[:, None] - idx[None, :]
    intra = jnp.where(diff >= 0, jnp.exp(jnp.maximum(diff, 0.0) * log_g[:, None, None]), 0.0)
    xi = jnp.exp((idx + 1.0)[None, :] * log_g[:, None])
    zeta = jnp.exp((chunk - 1.0 - idx)[None, :] * log_g[:, None])
    g_chunk = jnp.exp(chunk * log_g)
    eye = jnp.eye(reps, dtype=F32)
    intra_bd = jnp.einsum("ab,hij->haibj", eye, intra).reshape(h, reps * chunk, reps * chunk)
    n = reps * chunk
    xi_t = jnp.broadcast_to(jnp.tile(xi, (1, reps))[:, :, None], (h, n, RET_V_DIM))
    zeta_t = jnp.broadcast_to(jnp.tile(zeta, (1, reps))[:, :, None], (h, n, LANES))
    gdec = jnp.broadcast_to(jnp.repeat(g_chunk, RET_QK_DIM).reshape(h // 2, LANES, 1), (h // 2, LANES, RET_V_DIM))
    return intra_bd, xi_t, zeta_t, gdec


def _ret_prompt(rq, rk, rv, rg, b, s):
    tabs = _ret_tables(RET_CHUNK, 1)
    seq = lambda w: pl.BlockSpec((s, w), lambda i: (i, 0))
    return pl.pallas_call(
        _ret_prompt_kernel,
        grid=(b,),
        in_specs=[seq(RET_QK_WIDTH), seq(RET_QK_WIDTH), seq(RET_V_WIDTH), seq(RET_V_WIDTH)]
                 + [_const_spec(t.shape) for t in tabs],
        out_specs=[seq(RET_V_WIDTH),
                   pl.BlockSpec((None, N_RET_HEADS // 2, LANES, RET_V_DIM), lambda i: (i, 0, 0, 0))],
        out_shape=[jax.ShapeDtypeStruct((b * s, RET_V_WIDTH), BF16),
                   jax.ShapeDtypeStruct((b, N_RET_HEADS // 2, LANES, RET_V_DIM), F32)],
        scratch_shapes=[pltpu.VMEM((N_RET_HEADS // 2, LANES, RET_V_DIM), F32)],
        compiler_params=_cparams("arbitrary"),
        name="ret_prompt",
    )(rq, rk, rv, rg, *tabs)


SAMPLE_GROUP = 32


def _ret_sample_kernel(q_ref, k_ref, v_ref, g_ref, st_ref, intra_ref, xi_ref, zeta_ref, gdec_ref,
                       y_ref, so_ref, *, dec_seq):
    n = q_ref.shape[0]
    nb = n // dec_seq
    tok_seq = lax.broadcasted_iota(jnp.int32, (n, LANES), 0) // dec_seq
    seq_of_col = lax.broadcasted_iota(jnp.int32, (LANES, n), 1) // dec_seq
    for p in range(N_RET_HEADS // 2):
        pc = slice(p * LANES, (p + 1) * LANES)
        qp = q_ref[:, pc]
        kp = k_ref[:, pc]
        st = st_ref[:, p]
        st_flat = st.reshape(nb * LANES, RET_V_DIM).astype(BF16)
        upd = jnp.zeros((nb * LANES, RET_V_DIM), F32)
        for hh in range(2):
            h = 2 * p + hh
            hc = slice(h * RET_V_DIM, (h + 1) * RET_V_DIM)
            qm = qp * _head_mask(hh, n, BF16)
            s = lax.dot_general(qm, kp, _NT, preferred_element_type=F32) * intra_ref[h]
            vh = v_ref[:, hc]
            o = jnp.dot(s.astype(BF16), vh, preferred_element_type=F32)
            qf = qm.astype(F32)
            qexp = jnp.concatenate([jnp.where(tok_seq == b, qf, 0.0) for b in range(nb)], axis=1).astype(BF16)
            o = o + jnp.dot(qexp, st_flat, preferred_element_type=F32) * xi_ref[h]
            y_ref[:, hc] = (_rms(o) * _silu(g_ref[:, hc].astype(F32))).astype(BF16)
            kzt = (kp.astype(F32) * zeta_ref[h] * _head_mask(hh, n, F32)).T
            w = jnp.concatenate([jnp.where(seq_of_col == b, kzt, 0.0) for b in range(nb)], axis=0).astype(BF16)
            upd = upd + jnp.dot(w, vh, preferred_element_type=F32)
        so_ref[:, p] = st * gdec_ref[p] + upd.reshape(nb, LANES, RET_V_DIM)


def _ret_sample(rq, rk, rv, rg, state, dec_seq):
    b = state.shape[0]
    grp = SAMPLE_GROUP
    n = grp * dec_seq
    tabs = _ret_tables(dec_seq, grp)
    tok = lambda w: pl.BlockSpec((n, w), lambda i: (i, 0))
    st_spec = pl.BlockSpec((grp, N_RET_HEADS // 2, LANES, RET_V_DIM), lambda i: (i, 0, 0, 0))
    return pl.pallas_call(
        functools.partial(_ret_sample_kernel, dec_seq=dec_seq),
        grid=(b // grp,),
        in_specs=[tok(RET_QK_WIDTH), tok(RET_QK_WIDTH), tok(RET_V_WIDTH), tok(RET_V_WIDTH), st_spec]
                 + [_const_spec(t.shape) for t in tabs],
        out_specs=[tok(RET_V_WIDTH), st_spec],
        out_shape=[jax.ShapeDtypeStruct((b * dec_seq, RET_V_WIDTH), BF16),
                   jax.ShapeDtypeStruct(state.shape, F32)],
        compiler_params=_cparams("arbitrary"),
        name="ret_sample",
    )(rq, rk, rv, rg, state, *tabs)


def _attn_prompt_kernel(*refs, seq):
    n_pat = len(DILATIONS)
    in_refs = refs[:3 * n_pat]
    o_ref = refs[3 * n_pat]
    acc, m_s, l_s = refs[3 * n_pat + 1:]
    n_blk = seq // BLK

    row = lax.broadcasted_iota(jnp.int32, (2 * BLK, 2 * BLK), 0) & (BLK - 1)
    col = lax.broadcasted_iota(jnp.int32, (2 * BLK, 2 * BLK), 1)
    rel = row - col
    lo = lax.broadcasted_iota(jnp.int32, (BLK, LANES), 1) < HEAD_DIM
    mask_a, mask_b = _head_mask(0, BLK, BF16), _head_mask(1, BLK, BF16)

    for pi, d in enumerate(DILATIONS):
        q_ref, k_ref, v_ref = in_refs[3 * pi:3 * pi + 3]
        nb = seq // d // BLK
        first, last = pi == 0, pi == n_pat - 1
        assert d == 1 or not last

        def block(t, carry, d=d, nb=nb, first=first, last=last, q_ref=q_ref, k_ref=k_ref, v_ref=v_ref):
            i = t % nb
            r = t // nb
            base = pl.multiple_of(t * BLK, BLK)
            cur = pl.ds(base, BLK)
            win = pl.ds(pl.multiple_of(jnp.maximum(base - BLK, 0), BLK), 2 * BLK)
            dist = rel + jnp.where(t > 0, BLK, 0)
            valid = (dist >= 0) & (dist <= N_BACK) & (col >= jnp.where((i == 0) & (t > 0), BLK, 0))
            nat = cur if d == 1 else pl.ds(r + d * BLK * i, BLK, stride=d)
            for p in range(N_PAIRS):
                pc = slice(p * LANES, (p + 1) * LANES)
                qp = q_ref[cur, pc]
                q2 = jnp.concatenate([qp * mask_a, qp * mask_b], axis=0)
                s = lax.dot_general(q2, k_ref[win, pc], _NT, preferred_element_type=F32)
                s = jnp.where(valid, s, NEG)
                m = jnp.max(s, axis=-1, keepdims=True)
                e = jnp.exp2(s - m)
                l = jnp.sum(e, axis=-1, keepdims=True)
                o = jnp.dot(e.astype(BF16), v_ref[win, pc], preferred_element_type=F32)
                m_p = jnp.where(lo, m[:BLK], m[BLK:])
                l_p = jnp.where(lo, l[:BLK], l[BLK:])
                o_p = jnp.where(lo, o[:BLK], o[BLK:])
                if first:
                    m_s[p, nat, :], l_s[p, nat, :], acc[p, nat, :] = m_p, l_p, o_p
                    continue
                m_o = m_s[p, nat, :]
                m_n = jnp.maximum(m_o, m_p)
                a_o = jnp.exp2(m_o - m_n)
                a_p = jnp.exp2(m_p - m_n)
                l_n = a_o * l_s[p, nat, :] + a_p * l_p
                a_n = a_o * acc[p, nat, :] + a_p * o_p
                if last:
                    o_ref[cur, pc] = (a_n / l_n).astype(BF16)
                else:
                    m_s[p, nat, :], l_s[p, nat, :], acc[p, nat, :] = m_n, l_n, a_n
            return carry

        lax.fori_loop(0, n_blk, block, 0)


def _attn_prompt(qkv_by_dilation, b, s):
    seq_spec = pl.BlockSpec((s, ATT_WIDTH), lambda bi: (bi, 0))
    args = [a.reshape(b * s, ATT_WIDTH) for d in DILATIONS for a in qkv_by_dilation[d]]
    slab = pltpu.VMEM((N_PAIRS, s, LANES), F32)
    return pl.pallas_call(
        functools.partial(_attn_prompt_kernel, seq=s),
        grid=(b,),
        in_specs=[seq_spec] * len(args),
        out_specs=seq_spec,
        out_shape=jax.ShapeDtypeStruct((b * s, ATT_WIDTH), BF16),
        scratch_shapes=[slab, slab, slab],
        compiler_params=_cparams("arbitrary"),
        name="attn_prompt",
    )(*args)


def _attn_sample_kernel(q_ref, kn_ref, vn_ref, kc_ref, vc_ref, cnt_ref, o_ref, kx, vx, *, buf):
    ts = q_ref.shape[1]
    kx[...] = jnp.zeros(kx.shape, F32)
    vx[...] = jnp.zeros(vx.shape, F32)
    kx[:, :, 0:ts] = kn_ref[...]
    vx[:, :, 0:ts] = vn_ref[...]
    cnt_c, cnt_n = cnt_ref[:, 0:buf], cnt_ref[:, buf:]
    for h in range(N_ATT_HEADS):
        q = q_ref[h]
        s_c = jnp.where(cnt_c > 0.0, jnp.dot(q, kc_ref[h], preferred_element_type=F32), NEG)
        s_n = jnp.where(cnt_n > 0.0, jnp.dot(q, kx[h], preferred_element_type=F32), NEG)
        m = jnp.maximum(jnp.max(s_c, axis=-1, keepdims=True), jnp.max(s_n, axis=-1, keepdims=True))
        e_c = cnt_c * jnp.exp2(s_c - m)
        e_n = cnt_n * jnp.exp2(s_n - m)
        inv = 1.0 / (jnp.sum(e_c, axis=-1, keepdims=True) + jnp.sum(e_n, axis=-1, keepdims=True))
        o_ref[h] = (lax.dot_general(vc_ref[h], e_c * inv, _NT, preferred_element_type=F32)
                    + lax.dot_general(vx[h], e_n * inv, _NT, preferred_element_type=F32))


def _attn_sample(q, k_new, v_new, cache_k, cache_v):
    b, nh, ts, hd = q.shape
    buf = cache_k.shape[3]
    dist = buf + np.arange(ts)[:, None] - np.arange(buf + LANES)[None, :]
    cnt = np.zeros(dist.shape, np.float32)
    for d in DILATIONS:
        cnt += (dist >= 0) & (dist <= d * N_BACK) & (dist % d == 0)
    cnt[:, buf + ts:] = 0.0
    cnt = jnp.asarray(cnt)
    per_seq = lambda *shape: pl.BlockSpec((None,) + shape, lambda i: (i,) + (0,) * len(shape))
    return pl.pallas_call(
        functools.partial(_attn_sample_kernel, buf=buf),
        grid=(b,),
        in_specs=[per_seq(nh, ts, hd), per_seq(nh, hd, ts), per_seq(nh, hd, ts),
                  per_seq(nh, hd, buf), per_seq(nh, hd, buf), _const_spec(cnt.shape)],
        out_specs=per_seq(nh, hd, ts),
        out_shape=jax.ShapeDtypeStruct((b, nh, hd, ts), F32),
        scratch_shapes=[pltpu.VMEM((nh, hd, LANES), F32), pltpu.VMEM((nh, hd, LANES), F32)],
        compiler_params=_cparams("arbitrary"),
        name="attn_sample",
    )(q, k_new, v_new, cache_k, cache_v, cnt)


FF_CHUNK = 256


def _ffn_kernel(x_ref, ry_ref, ay_ref, g1_ref, sh2_ref, sc2_ref, g2_ref, n2_ref, fg_ref,
                wo_ref, wg_ref, wu_ref, wd_ref, y_ref):
    mix = jnp.dot(ry_ref[...], wo_ref[0:RET_V_WIDTH, :], preferred_element_type=F32)
    mix = mix + jnp.dot(ay_ref[...], wo_ref[RET_V_WIDTH:, :], preferred_element_type=F32)
    x1 = x_ref[...] + g1_ref[...] * mix
    h = _rms(x1) * n2_ref[...]
    h = (h * (1.0 + sc2_ref[...]) + sh2_ref[...]).astype(BF16)
    ff = jnp.zeros(x1.shape, F32)
    for c in range(D_FF // FF_CHUNK):
        cs = slice(c * FF_CHUNK, (c + 1) * FF_CHUNK)
        gate = jnp.dot(h, wg_ref[:, cs], preferred_element_type=F32)
        up = jnp.dot(h, wu_ref[:, cs], preferred_element_type=F32)
        ff = ff + jnp.dot((_silu(gate) * up).astype(BF16), wd_ref[cs, :], preferred_element_type=F32)
    x2 = x1 + g2_ref[...] * ff
    y_ref[...] = _rms(x2) * fg_ref[...]


def _ffn(x2, ret_y, att_y, mod_spec, mod_arr, n2, fg, wo, wg, wu, wd, tm):
    t, d = x2.shape
    row = lambda w: pl.BlockSpec((tm, w), lambda i: (i, 0))
    return pl.pallas_call(
        _ffn_kernel,
        grid=(t // tm,),
        in_specs=[row(d), row(RET_V_WIDTH), row(ATT_WIDTH), mod_spec(2), mod_spec(3), mod_spec(4), mod_spec(5),
                  _const_spec((1, d)), _const_spec((1, d)),
                  _const_spec(wo.shape), _const_spec(wg.shape), _const_spec(wu.shape), _const_spec(wd.shape)],
        out_specs=row(d),
        out_shape=jax.ShapeDtypeStruct((t, d), F32),
        compiler_params=_cparams("arbitrary"),
        name="outproj_ffn",
    )(x2, ret_y, att_y, mod_arr, mod_arr, mod_arr, mod_arr, n2, fg, wo, wg, wu, wd)


def _rope_tables(pos, inv_freq):
    ang = pos.astype(F32)[:, None] * inv_freq[None, :]
    cos, sin = jnp.cos(ang), jnp.sin(ang)
    reps = LANES // HEAD_DIM
    return (jnp.tile(jnp.concatenate([cos, cos], axis=1), (1, reps)),
            jnp.tile(jnp.concatenate([-sin, sin], axis=1), (1, reps)))


def _all_rope_tables(pos):
    half = jnp.arange(0, HEAD_DIM, 2, dtype=F32)
    ret_freq = ROPE_THETA ** (-jnp.linspace(0.0, 1.0, RET_QK_DIM // 2, dtype=F32))
    att_freq = ROPE_THETA ** (-half / HEAD_DIM)
    return _rope_tables(pos, ret_freq) + _rope_tables(pos, att_freq)


def kernel(x_prompt, x_sample, cache_attn_k, cache_attn_v, state_ret, c_prompt, c_sample,
           w_ada, b_ada, norm1_g, w_in, w_out, norm2_g, w_gate, w_up, w_down, final_g):
    b, s, d = x_prompt.shape
    bs, ts, _ = x_sample.shape
    depth = w_in.shape[0]
    assert depth == 1 and s % (BLK * max(DILATIONS)) == 0 and bs % SAMPLE_GROUP == 0

    xp = x_prompt.reshape(b * s, d)
    xs = x_sample.reshape(bs * ts, d)
    fg = final_g.reshape(1, d)
    kp_l, vp_l, rp_l, ks_l, vs_l, rs_l = [], [], [], [], [], []
    for l in range(depth):
        mod = _ada(jnp.concatenate([c_prompt, c_sample], axis=0), w_ada[l], b_ada[l])
        mod_p = mod[:b].reshape(b, 1, N_MOD * d)
        mod_s = jnp.repeat(mod[b:], ts, axis=0)
        w_in_b, wo_b = w_in[l].astype(BF16), w_out[l].astype(BF16)
        wg_b, wu_b, wd_b = w_gate[l].astype(BF16), w_up[l].astype(BF16), w_down[l].astype(BF16)
        g1, n2 = norm1_g[l].reshape(1, d), norm2_g[l].reshape(1, d)

        tm = 512
        per_seq = s // tm
        spec_p = lambda k: pl.BlockSpec((None, 1, d), lambda i, k=k: (i // per_seq, 0, k))
        tabs_p = _all_rope_tables(jnp.arange(s))
        outs = _inproj(xp, spec_p, mod_p, g1, w_in_b, tabs_p, tm, lambda i: (i % per_seq, 0), seq=s)
        rq, rk, rv, rg, ak, av = outs[:6]
        qkv = {1: outs[6:9]}
        for di, dd in enumerate(STRIDED):
            qkv[dd] = outs[9 + 3 * di:12 + 3 * di]
        ret_y, ret_state = _ret_prompt(rq, rk, rv, rg, b, s)
        att_y = _attn_prompt(qkv, b, s)
        xp = _ffn(xp, ret_y, att_y, spec_p, mod_p, n2, fg, wo_b, wg_b, wu_b, wd_b, tm)
        win = min(2048, s)
        to_rows = lambda a: jnp.transpose(a, (0, 3, 1, 2))
        kp_l.append(to_rows(ak)[:, s - win:])
        vp_l.append(to_rows(av)[:, s - win:])
        rp_l.append(ret_state.reshape(b, N_RET_HEADS, RET_QK_DIM, RET_V_DIM))

        n_tok = bs * ts
        tms = min(512, n_tok)
        spec_s = lambda k: pl.BlockSpec((tms, d), lambda i, k=k: (i, k))
        tabs_s = tuple(jnp.tile(tb, (n_tok // ts, 1)) for tb in _all_rope_tables(PAST_LEN + jnp.arange(ts)))
        rq, rk, rv, rg, ak, av, aq = _inproj(xs, spec_s, mod_s, g1, w_in_b, tabs_s, tms, lambda i: (i, 0))
        state2 = state_ret[l].reshape(bs, N_RET_HEADS // 2, LANES, RET_V_DIM)
        ret_y, ret_state = _ret_sample(rq, rk, rv, rg, state2, ts)
        by_seq = lambda a: a.reshape(bs, ts, N_ATT_HEADS, HEAD_DIM)
        cols = lambda a: jnp.transpose(a, (0, 2, 3, 1))
        att_t = _attn_sample(jnp.transpose(by_seq(aq), (0, 2, 1, 3)), cols(by_seq(ak)), cols(by_seq(av)),
                             cols(cache_attn_k[l]), cols(cache_attn_v[l]))
        att_y = to_rows(att_t).reshape(n_tok, ATT_WIDTH).astype(BF16)
        xs = _ffn(xs, ret_y, att_y, spec_s, mod_s, n2, fg, wo_b, wg_b, wu_b, wd_b, tms)
        ks_l.append(by_seq(ak))
        vs_l.append(by_seq(av))
        rs_l.append(ret_state.reshape(bs, N_RET_HEADS, RET_QK_DIM, RET_V_DIM))

    return (xp.reshape(b, s, d), xs.reshape(bs, ts, d),
            jnp.stack(kp_l, 0), jnp.stack(vp_l, 0), jnp.stack(rp_l, 0),
            jnp.stack(ks_l, 0), jnp.stack(vs_l, 0), jnp.stack(rs_l, 0))
```

```python
import functools

import jax
import jax.numpy as jnp
import numpy as np
from jax import lax
from jax.experimental import pallas as pl
from jax.experimental.pallas import tpu as pltpu

D_MODEL = 1024
HEAD_DIM = 64
N_ATT_HEADS = 8
ATT_WIDTH = N_ATT_HEADS * HEAD_DIM
ATT_SCALE = HEAD_DIM ** -0.5
LOG2E = 1.4426950408889634
DILATIONS = (16, 4, 1)
N_BACK = 128
ROPE_THETA = 10000.0
N_RET_HEADS = 4
RET_QK_DIM = 64
RET_V_DIM = 128
RET_QK_WIDTH = N_RET_HEADS * RET_QK_DIM
RET_V_WIDTH = N_RET_HEADS * RET_V_DIM
RET_CHUNK = 128
D_FF = 2816
N_MOD = 6
EPS = 1e-6
PAST_LEN = 8192

LANES = 128
BLK = 128
NEG = -1e30
VMEM_LIMIT = 56 * 1024 * 1024
BF16 = jnp.bfloat16
F32 = jnp.float32

_NT = (((1,), (1,)), ((), ()))
_TN = (((0,), (0,)), ((), ()))


def _cparams(*sem):
    return pltpu.CompilerParams(dimension_semantics=sem, vmem_limit_bytes=VMEM_LIMIT)


def _const_spec(shape):
    n = len(shape)
    return pl.BlockSpec(shape, lambda *_: (0,) * n, pipeline_mode=pl.Buffered(1))


def _head_mask(hh, rows, dtype):
    lane = lax.broadcasted_iota(jnp.int32, (rows, LANES), 1)
    own = (lane >= HEAD_DIM) if hh else (lane < HEAD_DIM)
    return jnp.where(own, 1.0, 0.0).astype(dtype)


def _rms(x):
    return x * lax.rsqrt(jnp.mean(x * x, axis=-1, keepdims=True) + EPS)


def _silu(x):
    return x * jax.nn.sigmoid(x)


def _ada_kernel(c_ref, w_ref, b_ref, o_ref):
    c = _silu(c_ref[...]).astype(BF16)
    o_ref[...] = jnp.dot(c, w_ref[...].astype(BF16), preferred_element_type=F32) + b_ref[...]


def _ada(c_all, w_ada, b_ada):
    n, d = c_all.shape
    width = w_ada.shape[1]
    tn = 1536
    return pl.pallas_call(
        _ada_kernel,
        grid=(width // tn,),
        in_specs=[pl.BlockSpec((n, d), lambda j: (0, 0)),
                  pl.BlockSpec((d, tn), lambda j: (0, j)),
                  pl.BlockSpec((1, tn), lambda j: (0, j))],
        out_specs=pl.BlockSpec((n, tn), lambda j: (0, j)),
        out_shape=jax.ShapeDtypeStruct((n, width), F32),
        compiler_params=_cparams("arbitrary"),
        name="ada_mod",
    )(c_all, w_ada, b_ada.reshape(1, width))


def _rotate(z, cos, sin_signed):
    lane = lax.broadcasted_iota(jnp.int32, (z.shape[0], LANES), 1)
    first_half = (lane & (HEAD_DIM - 1)) < (HEAD_DIM // 2)
    outs = []
    for c in range(z.shape[1] // LANES):
        zc = z[:, c * LANES:(c + 1) * LANES]
        partner = jnp.where(first_half, pltpu.roll(zc, LANES - HEAD_DIM // 2, 1), pltpu.roll(zc, HEAD_DIM // 2, 1))
        outs.append(zc * cos + partner * sin_signed)
    return jnp.concatenate(outs, axis=1)


_IN_OFFS = tuple(int(o) for o in np.cumsum((0, RET_QK_WIDTH, RET_QK_WIDTH, RET_V_WIDTH, RET_V_WIDTH,
                                            ATT_WIDTH, ATT_WIDTH, ATT_WIDTH)))


STRIDED = tuple(sorted(d for d in DILATIONS if d > 1))
STEP = 4
assert all(d == STEP ** (i + 1) for i, d in enumerate(STRIDED))
N_PAIRS = N_ATT_HEADS // 2
INPROJ_SUB = 256


def _inproj_kernel(*refs, prompt):
    x_ref, sh_ref, sc_ref, g_ref, w_ref, cr_ref, sr_ref, ca_ref, sa_ref = refs[:9]
    rq_ref, rk_ref, rv_ref, rg_ref = refs[9:13]
    ak_ref, av_ref = refs[13:15]
    tm = x_ref.shape[0]
    n_sub = tm // INPROJ_SUB if tm % INPROJ_SUB == 0 else 1
    ts = tm // n_sub
    for sub in range(n_sub):
        rs = slice(sub * ts, (sub + 1) * ts)
        per_row = lambda ref: ref[rs, :] if ref.shape[0] == tm else ref[...]
        h = _rms(x_ref[rs, :]) * g_ref[...]
        h = (h * (1.0 + per_row(sc_ref)) + per_row(sh_ref)).astype(BF16)

        def proj(i, h=h):
            return jnp.dot(h, w_ref[:, _IN_OFFS[i]:_IN_OFFS[i + 1]], preferred_element_type=F32)

        cr, sr, ca, sa = cr_ref[rs, :], sr_ref[rs, :], ca_ref[rs, :], sa_ref[rs, :]
        aq = _rotate(proj(4), ca, sa) * (ATT_SCALE * LOG2E)
        ak = _rotate(proj(5), ca, sa)
        av = proj(6)
        if prompt:
            ak_ref[:, :, rs] = ak.T.reshape(N_ATT_HEADS, HEAD_DIM, ts)
            av_ref[:, :, rs] = av.T.reshape(N_ATT_HEADS, HEAD_DIM, ts)
            nat_refs = refs[15:18]
            strided_refs = refs[18:18 + 3 * len(STRIDED)]
            slabs = refs[18 + 3 * len(STRIDED)]
            for a, z in enumerate((aq, ak, av)):
                nat_refs[a][rs, :] = z.astype(BF16)
                for p in range(N_PAIRS):
                    slabs[sub, 0, a, p] = z[:, p * LANES:(p + 1) * LANES]
                for lvl, d in enumerate(STRIDED):
                    o_ref = strided_refs[3 * lvl + a]
                    coarse = d // STEP
                    n = ts // d
                    for r_old in range(coarse):
                        for r_new in range(STEP):
                            r = r_new * coarse + r_old
                            for p in range(N_PAIRS):
                                v = slabs[sub, lvl, a, p, pl.ds(r_old * (ts // coarse) + r_new, n, stride=STEP), :]
                                o_ref[r, sub * n:(sub + 1) * n, p * LANES:(p + 1) * LANES] = v.astype(BF16)
                                if lvl + 1 < len(STRIDED):
                                    slabs[sub, lvl + 1, a, p, r * n:(r + 1) * n, :] = v
        else:
            for ref, z in zip((ak_ref, av_ref, refs[15]), (ak, av, aq)):
                ref[rs] = z.reshape(ts, N_ATT_HEADS, HEAD_DIM)
        rq_ref[rs, :] = _rotate(proj(0), cr, sr).astype(BF16)
        rk_ref[rs, :] = (_rotate(proj(1), cr, sr) * (RET_QK_DIM ** -0.5)).astype(BF16)
        rv_ref[rs, :] = proj(2).astype(BF16)
        rg_ref[rs, :] = proj(3).astype(BF16)


def _inproj(x2, mod_spec, mod_arr, g1, w_in_b, tabs, tm, tab_map, seq=None):
    t, d = x2.shape
    prompt = seq is not None
    row = lambda w: pl.BlockSpec((tm, w), lambda i: (i, 0))
    heads = pl.BlockSpec((tm, N_ATT_HEADS, HEAD_DIM), lambda i: (i, 0, 0))
    tab = pl.BlockSpec((tm, LANES), tab_map)
    heads_shape = jax.ShapeDtypeStruct((t, N_ATT_HEADS, HEAD_DIM), F32)
    out_specs = [row(RET_QK_WIDTH), row(RET_QK_WIDTH), row(RET_V_WIDTH), row(RET_V_WIDTH)]
    out_shape = [jax.ShapeDtypeStruct((t, w), BF16) for w in (RET_QK_WIDTH, RET_QK_WIDTH, RET_V_WIDTH, RET_V_WIDTH)]
    scratch = []
    if prompt:
        per_seq = seq // tm
        by_seq = lambda i: (i // per_seq, 0, 0, i % per_seq)
        out_specs += [pl.BlockSpec((None, N_ATT_HEADS, HEAD_DIM, tm), by_seq)] * 2
        out_shape += [jax.ShapeDtypeStruct((t // seq, N_ATT_HEADS, HEAD_DIM, seq), F32)] * 2
        out_specs += [row(ATT_WIDTH)] * 3
        out_shape += [jax.ShapeDtypeStruct((t, ATT_WIDTH), BF16)] * 3
        for dd in STRIDED:
            spec = pl.BlockSpec((None, dd, tm // dd, ATT_WIDTH), lambda i: (i // per_seq, 0, i % per_seq, 0))
            out_specs += [spec] * 3
            out_shape += [jax.ShapeDtypeStruct((t // seq, dd, seq // dd, ATT_WIDTH), BF16)] * 3
        assert tm % INPROJ_SUB == 0
        scratch = [pltpu.VMEM((tm // INPROJ_SUB, len(STRIDED), 3, N_PAIRS, INPROJ_SUB, LANES), F32)]
    else:
        out_specs += [heads] * 3
        out_shape += [heads_shape] * 3
    return pl.pallas_call(
        functools.partial(_inproj_kernel, prompt=prompt),
        grid=(t // tm,),
        in_specs=[row(d), mod_spec(0), mod_spec(1), _const_spec((1, d)), _const_spec(w_in_b.shape),
                  tab, tab, tab, tab],
        out_specs=out_specs,
        out_shape=out_shape,
        scratch_shapes=scratch,
        compiler_params=_cparams("arbitrary"),
        name="inproj_rope",
    )(x2, mod_arr, mod_arr, g1, w_in_b, *tabs)


def _ret_prompt_kernel(q_ref, k_ref, v_ref, g_ref, intra_ref, xi_ref, zeta_ref, gdec_ref,
                       y_ref, st_ref, r_ref):
    n_chunks = q_ref.shape[0] // RET_CHUNK
    r_ref[...] = jnp.zeros_like(r_ref)

    def chunk(c, carry):
        rows = pl.ds(pl.multiple_of(c * RET_CHUNK, RET_CHUNK), RET_CHUNK)
        for p in range(N_RET_HEADS // 2):
            pc = slice(p * LANES, (p + 1) * LANES)
            qp = q_ref[rows, pc]
            kp = k_ref[rows, pc]
            r_old = r_ref[p]
            r_new = r_old * gdec_ref[p]
            r_b = r_old.astype(BF16)
            for hh in range(2):
                h = 2 * p + hh
                hc = slice(h * RET_V_DIM, (h + 1) * RET_V_DIM)
                qm = qp * _head_mask(hh, RET_CHUNK, BF16)
                s = lax.dot_general(qm, kp, _NT, preferred_element_type=F32) * intra_ref[h]
                vh = v_ref[rows, hc]
                o = jnp.dot(s.astype(BF16), vh, preferred_element_type=F32)
                o = o + jnp.dot(qm, r_b, preferred_element_type=F32) * xi_ref[h]
                y_ref[rows, hc] = (_rms(o) * _silu(g_ref[rows, hc].astype(F32))).astype(BF16)
                kz = (kp.astype(F32) * zeta_ref[h] * _head_mask(hh, RET_CHUNK, F32)).astype(BF16)
                r_new = r_new + lax.dot_general(kz, vh, _TN, preferred_element_type=F32)
            r_ref[p] = r_new
        return carry

    lax.fori_loop(0, n_chunks, chunk, 0, unroll=8)
    st_ref[...] = r_ref[...]


def _ret_tables(chunk, reps):
    h = N_RET_HEADS
    log_g = jnp.log1p(-(2.0 ** (-5.0 - jnp.arange(h, dtype=F32))))
    idx = jnp.arange(chunk, dtype=F32)
    diff = idx[:, None] - idx[None, :]
    intra = jnp.where(diff >= 0, jnp.exp(jnp.maximum(diff, 0.0) * log_g[:, None, None]), 0.0)
    xi = jnp.exp((idx + 1.0)[None, :] * log_g[:, None])
    zeta = jnp.exp((chunk - 1.0 - idx)[None, :] * log_g[:, None])
    g_chunk = jnp.exp(chunk * log_g)
    eye = jnp.eye(reps, dtype=F32)
    intra_bd = jnp.einsum("ab,hij->haibj", eye, intra).reshape(h, reps * chunk, reps * chunk)
    n = reps * chunk
    xi_t = jnp.broadcast_to(jnp.tile(xi, (1, reps))[:, :, None], (h, n, RET_V_DIM))
    zeta_t = jnp.broadcast_to(jnp.tile(zeta, (1, reps))[:, :, None], (h, n, LANES))
    gdec = jnp.broadcast_to(jnp.repeat(g_chunk, RET_QK_DIM).reshape(h // 2, LANES, 1), (h // 2, LANES, RET_V_DIM))
    return intra_bd, xi_t, zeta_t, gdec


def _ret_prompt(rq, rk, rv, rg, b, s):
    tabs = _ret_tables(RET_CHUNK, 1)
    seq = lambda w: pl.BlockSpec((s, w), lambda i: (i, 0))
    return pl.pallas_call(
        _ret_prompt_kernel,
        grid=(b,),
        in_specs=[seq(RET_QK_WIDTH), seq(RET_QK_WIDTH), seq(RET_V_WIDTH), seq(RET_V_WIDTH)]
                 + [_const_spec(t.shape) for t in tabs],
        out_specs=[seq(RET_V_WIDTH),
                   pl.BlockSpec((None, N_RET_HEADS // 2, LANES, RET_V_DIM), lambda i: (i, 0, 0, 0))],
        out_shape=[jax.ShapeDtypeStruct((b * s, RET_V_WIDTH), BF16),
                   jax.ShapeDtypeStruct((b, N_RET_HEADS // 2, LANES, RET_V_DIM), F32)],
        scratch_shapes=[pltpu.VMEM((N_RET_HEADS // 2, LANES, RET_V_DIM), F32)],
        compiler_params=_cparams("arbitrary"),
        name="ret_prompt",
    )(rq, rk, rv, rg, *tabs)


SAMPLE_GROUP = 32


def _ret_sample_kernel(q_ref, k_ref, v_ref, g_ref, st_ref, intra_ref, xi_ref, zeta_ref, gdec_ref,
                       y_ref, so_ref, *, dec_seq):
    n = q_ref.shape[0]
    nb = n // dec_seq
    tok_seq = lax.broadcasted_iota(jnp.int32, (n, LANES), 0) // dec_seq
    seq_of_col = lax.broadcasted_iota(jnp.int32, (LANES, n), 1) // dec_seq
    for p in range(N_RET_HEADS // 2):
        pc = slice(p * LANES, (p + 1) * LANES)
        qp = q_ref[:, pc]
        kp = k_ref[:, pc]
        st = st_ref[:, p]
        st_flat = st.reshape(nb * LANES, RET_V_DIM).astype(BF16)
        upd = jnp.zeros((nb * LANES, RET_V_DIM), F32)
        for hh in range(2):
            h = 2 * p + hh
            hc = slice(h * RET_V_DIM, (h + 1) * RET_V_DIM)
            qm = qp * _head_mask(hh, n, BF16)
            s = lax.dot_general(qm, kp, _NT, preferred_element_type=F32) * intra_ref[h]
            vh = v_ref[:, hc]
            o = jnp.dot(s.astype(BF16), vh, preferred_element_type=F32)
            qf = qm.astype(F32)
            qexp = jnp.concatenate([jnp.where(tok_seq == b, qf, 0.0) for b in range(nb)], axis=1).astype(BF16)
            o = o + jnp.dot(qexp, st_flat, preferred_element_type=F32) * xi_ref[h]
            y_ref[:, hc] = (_rms(o) * _silu(g_ref[:, hc].astype(F32))).astype(BF16)
            kzt = (kp.astype(F32) * zeta_ref[h] * _head_mask(hh, n, F32)).T
            w = jnp.concatenate([jnp.where(seq_of_col == b, kzt, 0.0) for b in range(nb)], axis=0).astype(BF16)
            upd = upd + jnp.dot(w, vh, preferred_element_type=F32)
        so_ref[:, p] = st * gdec_ref[p] + upd.reshape(nb, LANES, RET_V_DIM)


def _ret_sample(rq, rk, rv, rg, state, dec_seq):
    b = state.shape[0]
    grp = SAMPLE_GROUP
    n = grp * dec_seq
    tabs = _ret_tables(dec_seq, grp)
    tok = lambda w: pl.BlockSpec((n, w), lambda i: (i, 0))
    st_spec = pl.BlockSpec((grp, N_RET_HEADS // 2, LANES, RET_V_DIM), lambda i: (i, 0, 0, 0))
    return pl.pallas_call(
        functools.partial(_ret_sample_kernel, dec_seq=dec_seq),
        grid=(b // grp,),
        in_specs=[tok(RET_QK_WIDTH), tok(RET_QK_WIDTH), tok(RET_V_WIDTH), tok(RET_V_WIDTH), st_spec]
                 + [_const_spec(t.shape) for t in tabs],
        out_specs=[tok(RET_V_WIDTH), st_spec],
        out_shape=[jax.ShapeDtypeStruct((b * dec_seq, RET_V_WIDTH), BF16),
                   jax.ShapeDtypeStruct(state.shape, F32)],
        compiler_params=_cparams("arbitrary"),
        name="ret_sample",
    )(rq, rk, rv, rg, state, *tabs)


def _attn_prompt_kernel(*refs, seq):
    n_pat = len(DILATIONS)
    in_refs = refs[:3 * n_pat]
    o_ref = refs[3 * n_pat]
    acc, m_s, l_s = refs[3 * n_pat + 1:]
    n_blk = seq // BLK

    lo = lax.broadcasted_iota(jnp.int32, (BLK, LANES), 1) < HEAD_DIM
    mask_a, mask_b = _head_mask(0, BLK, BF16), _head_mask(1, BLK, BF16)

    for pi, d in enumerate(DILATIONS):
        q_ref, k_ref, v_ref = in_refs[3 * pi:3 * pi + 3]
        nb = seq // d // BLK
        first, last = pi == 0, pi == n_pat - 1
        assert d == 1 or not last
        n_win = 1 if nb == 1 else 2
        row = lax.broadcasted_iota(jnp.int32, (2 * BLK, n_win * BLK), 0) & (BLK - 1)
        col = lax.broadcasted_iota(jnp.int32, (2 * BLK, n_win * BLK), 1)
        rel = row - col

        def block(t, carry, d=d, nb=nb, first=first, last=last, q_ref=q_ref, k_ref=k_ref, v_ref=v_ref,
                  n_win=n_win, rel=rel, col=col):
            i = t % nb
            r = t // nb
            base = pl.multiple_of(t * BLK, BLK)
            cur = pl.ds(base, BLK)
            if n_win == 1:
                win = cur
                valid = rel >= 0
            else:
                win = pl.ds(pl.multiple_of(jnp.maximum(base - BLK, 0), BLK), 2 * BLK)
                dist = rel + jnp.where(t > 0, BLK, 0)
                valid = (dist >= 0) & (dist <= N_BACK) & (col >= jnp.where((i == 0) & (t > 0), BLK, 0))
            nat = cur if d == 1 else pl.ds(r + d * BLK * i, BLK, stride=d)
            for p in range(N_PAIRS):
                pc = slice(p * LANES, (p + 1) * LANES)
                qp = q_ref[cur, pc]
                q2 = jnp.concatenate([qp * mask_a, qp * mask_b], axis=0)
                s = lax.dot_general(q2, k_ref[win, pc], _NT, preferred_element_type=F32)
                s = jnp.where(valid, s, NEG)
                m = jnp.max(s, axis=-1, keepdims=True)
                e = jnp.exp2(s - m)
                l = jnp.sum(e, axis=-1, keepdims=True)
                o = jnp.dot(e.astype(BF16), v_ref[win, pc], preferred_element_type=F32)
                m_p = jnp.where(lo, m[:BLK], m[BLK:])
                l_p = jnp.where(lo, l[:BLK], l[BLK:])
                o_p = jnp.where(lo, o[:BLK], o[BLK:])
                if first:
                    m_s[p, nat, :], l_s[p, nat, :], acc[p, nat, :] = m_p, l_p, o_p
                    continue
                m_o = m_s[p, nat, :]
                m_n = jnp.maximum(m_o, m_p)
                a_o = jnp.exp2(m_o - m_n)
                a_p = jnp.exp2(m_p - m_n)
                l_n = a_o * l_s[p, nat, :] + a_p * l_p
                a_n = a_o * acc[p, nat, :] + a_p * o_p
                if last:
                    o_ref[cur, pc] = (a_n / l_n).astype(BF16)
                else:
                    m_s[p, nat, :], l_s[p, nat, :], acc[p, nat, :] = m_n, l_n, a_n
            return carry

        lax.fori_loop(0, n_blk, block, 0, unroll=4)


def _attn_prompt(qkv_by_dilation, b, s):
    seq_spec = pl.BlockSpec((s, ATT_WIDTH), lambda bi: (bi, 0))
    args = [a.reshape(b * s, ATT_WIDTH) for d in DILATIONS for a in qkv_by_dilation[d]]
    slab = pltpu.VMEM((N_PAIRS, s, LANES), F32)
    return pl.pallas_call(
        functools.partial(_attn_prompt_kernel, seq=s),
        grid=(b,),
        in_specs=[seq_spec] * len(args),
        out_specs=seq_spec,
        out_shape=jax.ShapeDtypeStruct((b * s, ATT_WIDTH), BF16),
        scratch_shapes=[slab, slab, slab],
        compiler_params=_cparams("arbitrary"),
        name="attn_prompt",
    )(*args)


def _attn_sample_kernel(q_ref, kn_ref, vn_ref, kc_ref, vc_ref, cnt_ref, o_ref, kx, vx, *, buf):
    ts = q_ref.shape[1]
    kx[...] = jnp.zeros(kx.shape, F32)
    vx[...] = jnp.zeros(vx.shape, F32)
    kx[:, :, 0:ts] = kn_ref[...]
    vx[:, :, 0:ts] = vn_ref[...]
    cnt_c, cnt_n = cnt_ref[:, 0:buf], cnt_ref[:, buf:]
    for h in range(N_ATT_HEADS):
        q = q_ref[h]
        s_c = jnp.where(cnt_c > 0.0, jnp.dot(q, kc_ref[h], preferred_element_type=F32), NEG)
        s_n = jnp.where(cnt_n > 0.0, jnp.dot(q, kx[h], preferred_element_type=F32), NEG)
        m = jnp.maximum(jnp.max(s_c, axis=-1, keepdims=True), jnp.max(s_n, axis=-1, keepdims=True))
        e_c = cnt_c * jnp.exp2(s_c - m)
        e_n = cnt_n * jnp.exp2(s_n - m)
        inv = 1.0 / (jnp.sum(e_c, axis=-1, keepdims=True) + jnp.sum(e_n, axis=-1, keepdims=True))
        o_ref[h] = (lax.dot_general(vc_ref[h], e_c * inv, _NT, preferred_element_type=F32)
                    + lax.dot_general(vx[h], e_n * inv, _NT, preferred_element_type=F32))


def _attn_sample(q, k_new, v_new, cache_k, cache_v):
    b, nh, ts, hd = q.shape
    buf = cache_k.shape[3]
    dist = buf + np.arange(ts)[:, None] - np.arange(buf + LANES)[None, :]
    cnt = np.zeros(dist.shape, np.float32)
    for d in DILATIONS:
        cnt += (dist >= 0) & (dist <= d * N_BACK) & (dist % d == 0)
    cnt[:, buf + ts:] = 0.0
    cnt = jnp.asarray(cnt)
    per_seq = lambda *shape: pl.BlockSpec((None,) + shape, lambda i: (i,) + (0,) * len(shape))
    return pl.pallas_call(
        functools.partial(_attn_sample_kernel, buf=buf),
        grid=(b,),
        in_specs=[per_seq(nh, ts, hd), per_seq(nh, hd, ts), per_seq(nh, hd, ts),
                  per_seq(nh, hd, buf), per_seq(nh, hd, buf), _const_spec(cnt.shape)],
        out_specs=per_seq(nh, hd, ts),
        out_shape=jax.ShapeDtypeStruct((b, nh, hd, ts), F32),
        scratch_shapes=[pltpu.VMEM((nh, hd, LANES), F32), pltpu.VMEM((nh, hd, LANES), F32)],
        compiler_params=_cparams("arbitrary"),
        name="attn_sample",
    )(q, k_new, v_new, cache_k, cache_v, cnt)


FF_CHUNK = 256


def _ffn_kernel(x_ref, ry_ref, ay_ref, g1_ref, sh2_ref, sc2_ref, g2_ref, n2_ref, fg_ref,
                wo_ref, wg_ref, wu_ref, wd_ref, y_ref):
    mix = jnp.dot(ry_ref[...], wo_ref[0:RET_V_WIDTH, :], preferred_element_type=F32)
    mix = mix + jnp.dot(ay_ref[...], wo_ref[RET_V_WIDTH:, :], preferred_element_type=F32)
    x1 = x_ref[...] + g1_ref[...] * mix
    h = _rms(x1) * n2_ref[...]
    h = (h * (1.0 + sc2_ref[...]) + sh2_ref[...]).astype(BF16)
    ff = jnp.zeros(x1.shape, F32)
    for c in range(D_FF // FF_CHUNK):
        cs = slice(c * FF_CHUNK, (c + 1) * FF_CHUNK)
        gate = jnp.dot(h, wg_ref[:, cs], preferred_element_type=F32)
        up = jnp.dot(h, wu_ref[:, cs], preferred_element_type=F32)
        ff = ff + jnp.dot((_silu(gate) * up).astype(BF16), wd_ref[cs, :], preferred_element_type=F32)
    x2 = x1 + g2_ref[...] * ff
    y_ref[...] = _rms(x2) * fg_ref[...]


def _ffn(x2, ret_y, att_y, mod_spec, mod_arr, n2, fg, wo, wg, wu, wd, tm):
    t, d = x2.shape
    row = lambda w: pl.BlockSpec((tm, w), lambda i: (i, 0))
    return pl.pallas_call(
        _ffn_kernel,
        grid=(t // tm,),
        in_specs=[row(d), row(RET_V_WIDTH), row(ATT_WIDTH), mod_spec(2), mod_spec(3), mod_spec(4), mod_spec(5),
                  _const_spec((1, d)), _const_spec((1, d)),
                  _const_spec(wo.shape), _const_spec(wg.shape), _const_spec(wu.shape), _const_spec(wd.shape)],
        out_specs=row(d),
        out_shape=jax.ShapeDtypeStruct((t, d), F32),
        compiler_params=_cparams("arbitrary"),
        name="outproj_ffn",
    )(x2, ret_y, att_y, mod_arr, mod_arr, mod_arr, mod_arr, n2, fg, wo, wg, wu, wd)


def _rope_tables(pos, inv_freq):
    ang = pos.astype(F32)[:, None] * inv_freq[None, :]
    cos, sin = jnp.cos(ang), jnp.sin(ang)
    reps = LANES // HEAD_DIM
    return (jnp.tile(jnp.concatenate([cos, cos], axis=1), (1, reps)),
            jnp.tile(jnp.concatenate([-sin, sin], axis=1), (1, reps)))


def _all_rope_tables(pos):
    half = jnp.arange(0, HEAD_DIM, 2, dtype=F32)
    ret_freq = ROPE_THETA ** (-jnp.linspace(0.0, 1.0, RET_QK_DIM // 2, dtype=F32))
    att_freq = ROPE_THETA ** (-half / HEAD_DIM)
    return _rope_tables(pos, ret_freq) + _rope_tables(pos, att_freq)


def kernel(x_prompt, x_sample, cache_attn_k, cache_attn_v, state_ret, c_prompt, c_sample,
           w_ada, b_ada, norm1_g, w_in, w_out, norm2_g, w_gate, w_up, w_down, final_g):
    b, s, d = x_prompt.shape
    bs, ts, _ = x_sample.shape
    depth = w_in.shape[0]
    assert depth == 1 and s % (BLK * max(DILATIONS)) == 0 and bs % SAMPLE_GROUP == 0

    xp = x_prompt.reshape(b * s, d)
    xs = x_sample.reshape(bs * ts, d)
    fg = final_g.reshape(1, d)
    kp_l, vp_l, rp_l, ks_l, vs_l, rs_l = [], [], [], [], [], []
    for l in range(depth):
        mod = _ada(jnp.concatenate([c_prompt, c_sample], axis=0), w_ada[l], b_ada[l])
        mod_p = mod[:b].reshape(b, 1, N_MOD * d)
        mod_s = jnp.repeat(mod[b:], ts, axis=0)
        w_in_b, wo_b = w_in[l].astype(BF16), w_out[l].astype(BF16)
        wg_b, wu_b, wd_b = w_gate[l].astype(BF16), w_up[l].astype(BF16), w_down[l].astype(BF16)
        g1, n2 = norm1_g[l].reshape(1, d), norm2_g[l].reshape(1, d)

        tm = 512
        per_seq = s // tm
        spec_p = lambda k: pl.BlockSpec((None, 1, d), lambda i, k=k: (i // per_seq, 0, k))
        tabs_p = _all_rope_tables(jnp.arange(s))
        outs = _inproj(xp, spec_p, mod_p, g1, w_in_b, tabs_p, tm, lambda i: (i % per_seq, 0), seq=s)
        rq, rk, rv, rg, ak, av = outs[:6]
        qkv = {1: outs[6:9]}
        for di, dd in enumerate(STRIDED):
            qkv[dd] = outs[9 + 3 * di:12 + 3 * di]
        ret_y, ret_state = _ret_prompt(rq, rk, rv, rg, b, s)
        att_y = _attn_prompt(qkv, b, s)
        xp = _ffn(xp, ret_y, att_y, spec_p, mod_p, n2, fg, wo_b, wg_b, wu_b, wd_b, tm)
        win = min(2048, s)
        to_rows = lambda a: jnp.transpose(a, (0, 3, 1, 2))
        kp_l.append(to_rows(ak)[:, s - win:])
        vp_l.append(to_rows(av)[:, s - win:])
        rp_l.append(ret_state.reshape(b, N_RET_HEADS, RET_QK_DIM, RET_V_DIM))

        n_tok = bs * ts
        tms = min(512, n_tok)
        spec_s = lambda k: pl.BlockSpec((tms, d), lambda i, k=k: (i, k))
        tabs_s = tuple(jnp.tile(tb, (n_tok // ts, 1)) for tb in _all_rope_tables(PAST_LEN + jnp.arange(ts)))
        rq, rk, rv, rg, ak, av, aq = _inproj(xs, spec_s, mod_s, g1, w_in_b, tabs_s, tms, lambda i: (i, 0))
        state2 = state_ret[l].reshape(bs, N_RET_HEADS // 2, LANES, RET_V_DIM)
        ret_y, ret_state = _ret_sample(rq, rk, rv, rg, state2, ts)
        by_seq = lambda a: a.reshape(bs, ts, N_ATT_HEADS, HEAD_DIM)
        cols = lambda a: jnp.transpose(a, (0, 2, 3, 1))
        att_t = _attn_sample(jnp.transpose(by_seq(aq), (0, 2, 1, 3)), cols(by_seq(ak)), cols(by_seq(av)),
                             cols(cache_attn_k[l]), cols(cache_attn_v[l]))
        att_y = to_rows(att_t).reshape(n_tok, ATT_WIDTH).astype(BF16)
        xs = _ffn(xs, ret_y, att_y, spec_s, mod_s, n2, fg, wo_b, wg_b, wu_b, wd_b, tms)
        ks_l.append(by_seq(ak))
        vs_l.append(by_seq(av))
        rs_l.append(ret_state.reshape(bs, N_RET_HEADS, RET_QK_DIM, RET_V_DIM))

    return (xp.reshape(b, s, d), xs.reshape(bs, ts, d),
            jnp.stack(kp_l, 0), jnp.stack(vp_l, 0), jnp.stack(rp_l, 0),
            jnp.stack(ks_l, 0), jnp.stack(vs_l, 0), jnp.stack(rs_l, 0))
```

```python
import functools

import jax
import jax.numpy as jnp
import numpy as np
from jax import lax
from jax.experimental import pallas as pl
from jax.experimental.pallas import tpu as pltpu

D_MODEL = 1024
HEAD_DIM = 64
N_ATT_HEADS = 8
ATT_WIDTH = N_ATT_HEADS * HEAD_DIM
ATT_SCALE = HEAD_DIM ** -0.5
LOG2E = 1.4426950408889634
DILATIONS = (16, 4, 1)
N_BACK = 128
ROPE_THETA = 10000.0
N_RET_HEADS = 4
RET_QK_DIM = 64
RET_V_DIM = 128
RET_QK_WIDTH = N_RET_HEADS * RET_QK_DIM
RET_V_WIDTH = N_RET_HEADS * RET_V_DIM
RET_CHUNK = 128
D_FF = 2816
N_MOD = 6
EPS = 1e-6
PAST_LEN = 8192

LANES = 128
BLK = 128
NEG = -1e30
VMEM_LIMIT = 56 * 1024 * 1024
BF16 = jnp.bfloat16
F32 = jnp.float32

_NT = (((1,), (1,)), ((), ()))
_TN = (((0,), (0,)), ((), ()))


def _cparams(*sem):
    return pltpu.CompilerParams(dimension_semantics=sem, vmem_limit_bytes=VMEM_LIMIT)


def _const_spec(shape):
    n = len(shape)
    return pl.BlockSpec(shape, lambda *_: (0,) * n, pipeline_mode=pl.Buffered(1))


def _head_mask(hh, rows, dtype):
    lane = lax.broadcasted_iota(jnp.int32, (rows, LANES), 1)
    own = (lane >= HEAD_DIM) if hh else (lane < HEAD_DIM)
    return jnp.where(own, 1.0, 0.0).astype(dtype)


def _rms(x):
    return x * lax.rsqrt(jnp.mean(x * x, axis=-1, keepdims=True) + EPS)


def _silu(x):
    return x * jax.nn.sigmoid(x)


def _ada_kernel(c_ref, w_ref, b_ref, o_ref):
    c = _silu(c_ref[...]).astype(BF16)
    o_ref[...] = jnp.dot(c, w_ref[...].astype(BF16), preferred_element_type=F32) + b_ref[...]


def _ada(c_all, w_ada, b_ada):
    n, d = c_all.shape
    width = w_ada.shape[1]
    tn = 1536
    return pl.pallas_call(
        _ada_kernel,
        grid=(width // tn,),
        in_specs=[pl.BlockSpec((n, d), lambda j: (0, 0)),
                  pl.BlockSpec((d, tn), lambda j: (0, j)),
                  pl.BlockSpec((1, tn), lambda j: (0, j))],
        out_specs=pl.BlockSpec((n, tn), lambda j: (0, j)),
        out_shape=jax.ShapeDtypeStruct((n, width), F32),
        compiler_params=_cparams("arbitrary"),
        name="ada_mod",
    )(c_all, w_ada, b_ada.reshape(1, width))


def _rotate(z, cos, sin_signed):
    lane = lax.broadcasted_iota(jnp.int32, (z.shape[0], LANES), 1)
    first_half = (lane & (HEAD_DIM - 1)) < (HEAD_DIM // 2)
    outs = []
    for c in range(z.shape[1] // LANES):
        zc = z[:, c * LANES:(c + 1) * LANES]
        partner = jnp.where(first_half, pltpu.roll(zc, LANES - HEAD_DIM // 2, 1), pltpu.roll(zc, HEAD_DIM // 2, 1))
        outs.append(zc * cos + partner * sin_signed)
    return jnp.concatenate(outs, axis=1)


_IN_OFFS = tuple(int(o) for o in np.cumsum((0, RET_QK_WIDTH, RET_QK_WIDTH, RET_V_WIDTH, RET_V_WIDTH,
                                            ATT_WIDTH, ATT_WIDTH, ATT_WIDTH)))


STRIDED = tuple(sorted(d for d in DILATIONS if d > 1))
STEP = 4
assert all(d == STEP ** (i + 1) for i, d in enumerate(STRIDED))
N_PAIRS = N_ATT_HEADS // 2
INPROJ_SUB = 256


def _inproj_kernel(*refs, prompt):
    x_ref, sh_ref, sc_ref, g_ref, w_ref, cr_ref, sr_ref, ca_ref, sa_ref = refs[:9]
    rq_ref, rk_ref, rv_ref, rg_ref = refs[9:13]
    ak_ref, av_ref = refs[13:15]
    tm = x_ref.shape[0]
    n_sub = tm // INPROJ_SUB if tm % INPROJ_SUB == 0 else 1
    ts = tm // n_sub
    for sub in range(n_sub):
        rs = slice(sub * ts, (sub + 1) * ts)
        per_row = lambda ref: ref[rs, :] if ref.shape[0] == tm else ref[...]
        h = _rms(x_ref[rs, :]) * g_ref[...]
        h = (h * (1.0 + per_row(sc_ref)) + per_row(sh_ref)).astype(BF16)

        def proj(i, h=h):
            return jnp.dot(h, w_ref[:, _IN_OFFS[i]:_IN_OFFS[i + 1]], preferred_element_type=F32)

        cr, sr, ca, sa = cr_ref[rs, :], sr_ref[rs, :], ca_ref[rs, :], sa_ref[rs, :]
        aq = _rotate(proj(4), ca, sa) * (ATT_SCALE * LOG2E)
        ak = _rotate(proj(5), ca, sa)
        av = proj(6)
        if prompt:
            ak_ref[:, :, rs] = ak.T.reshape(N_ATT_HEADS, HEAD_DIM, ts)
            av_ref[:, :, rs] = av.T.reshape(N_ATT_HEADS, HEAD_DIM, ts)
            nat_refs = refs[15:18]
            strided_refs = refs[18:18 + 3 * len(STRIDED)]
            slabs = refs[18 + 3 * len(STRIDED)]
            for a, z in enumerate((aq, ak, av)):
                nat_refs[a][rs, :] = z.astype(BF16)
                for p in range(N_PAIRS):
                    slabs[sub, 0, a, p] = z[:, p * LANES:(p + 1) * LANES]
                for lvl, d in enumerate(STRIDED):
                    o_ref = strided_refs[3 * lvl + a]
                    coarse = d // STEP
                    n = ts // d
                    for r_old in range(coarse):
                        for r_new in range(STEP):
                            r = r_new * coarse + r_old
                            for p in range(N_PAIRS):
                                v = slabs[sub, lvl, a, p, pl.ds(r_old * (ts // coarse) + r_new, n, stride=STEP), :]
                                o_ref[r, sub * n:(sub + 1) * n, p * LANES:(p + 1) * LANES] = v.astype(BF16)
                                if lvl + 1 < len(STRIDED):
                                    slabs[sub, lvl + 1, a, p, r * n:(r + 1) * n, :] = v
        else:
            for ref, z in zip((ak_ref, av_ref, refs[15]), (ak, av, aq)):
                ref[rs] = z.reshape(ts, N_ATT_HEADS, HEAD_DIM)
        rq_ref[rs, :] = _rotate(proj(0), cr, sr).astype(BF16)
        rk_ref[rs, :] = (_rotate(proj(1), cr, sr) * (RET_QK_DIM ** -0.5)).astype(BF16)
        rv_ref[rs, :] = proj(2).astype(BF16)
        rg_ref[rs, :] = proj(3).astype(BF16)


def _inproj(x2, mod_spec, mod_arr, g1, w_in_b, tabs, tm, tab_map, seq=None):
    t, d = x2.shape
    prompt = seq is not None
    row = lambda w: pl.BlockSpec((tm, w), lambda i: (i, 0))
    heads = pl.BlockSpec((tm, N_ATT_HEADS, HEAD_DIM), lambda i: (i, 0, 0))
    tab = pl.BlockSpec((tm, LANES), tab_map)
    heads_shape = jax.ShapeDtypeStruct((t, N_ATT_HEADS, HEAD_DIM), F32)
    out_specs = [row(RET_QK_WIDTH), row(RET_QK_WIDTH), row(RET_V_WIDTH), row(RET_V_WIDTH)]
    out_shape = [jax.ShapeDtypeStruct((t, w), BF16) for w in (RET_QK_WIDTH, RET_QK_WIDTH, RET_V_WIDTH, RET_V_WIDTH)]
    scratch = []
    if prompt:
        per_seq = seq // tm
        by_seq = lambda i: (i // per_seq, 0, 0, i % per_seq)
        out_specs += [pl.BlockSpec((None, N_ATT_HEADS, HEAD_DIM, tm), by_seq)] * 2
        out_shape += [jax.ShapeDtypeStruct((t // seq, N_ATT_HEADS, HEAD_DIM, seq), F32)] * 2
        out_specs += [row(ATT_WIDTH)] * 3
        out_shape += [jax.ShapeDtypeStruct((t, ATT_WIDTH), BF16)] * 3
        for dd in STRIDED:
            spec = pl.BlockSpec((None, dd, tm // dd, ATT_WIDTH), lambda i: (i // per_seq, 0, i % per_seq, 0))
            out_specs += [spec] * 3
            out_shape += [jax.ShapeDtypeStruct((t // seq, dd, seq // dd, ATT_WIDTH), BF16)] * 3
        assert tm % INPROJ_SUB == 0
        scratch = [pltpu.VMEM((tm // INPROJ_SUB, len(STRIDED), 3, N_PAIRS, INPROJ_SUB, LANES), F32)]
    else:
        out_specs += [heads] * 3
        out_shape += [heads_shape] * 3
    return pl.pallas_call(
        functools.partial(_inproj_kernel, prompt=prompt),
        grid=(t // tm,),
        in_specs=[row(d), mod_spec(0), mod_spec(1), _const_spec((1, d)), _const_spec(w_in_b.shape),
                  tab, tab, tab, tab],
        out_specs=out_specs,
        out_shape=out_shape,
        scratch_shapes=scratch,
        compiler_params=_cparams("arbitrary"),
        name="inproj_rope",
    )(x2, mod_arr, mod_arr, g1, w_in_b, *tabs)


def _ret_prompt_kernel(q_ref, k_ref, v_ref, g_ref, intra_ref, xi_ref, zeta_ref, gdec_ref,
                       y_ref, st_ref, r_ref):
    n_chunks = q_ref.shape[0] // RET_CHUNK
    r_ref[...] = jnp.zeros_like(r_ref)

    def chunk(c, carry):
        rows = pl.ds(pl.multiple_of(c * RET_CHUNK, RET_CHUNK), RET_CHUNK)
        for p in range(N_RET_HEADS // 2):
            pc = slice(p * LANES, (p + 1) * LANES)
            qp = q_ref[rows, pc]
            kp = k_ref[rows, pc]
            r_old = r_ref[p]
            r_new = r_old * gdec_ref[p]
            r_b = r_old.astype(BF16)
            for hh in range(2):
                h = 2 * p + hh
                hc = slice(h * RET_V_DIM, (h + 1) * RET_V_DIM)
                qm = qp * _head_mask(hh, RET_CHUNK, BF16)
                s = lax.dot_general(qm, kp, _NT, preferred_element_type=F32) * intra_ref[h]
                vh = v_ref[rows, hc]
                o = jnp.dot(s.astype(BF16), vh, preferred_element_type=F32)
                o = o + jnp.dot(qm, r_b, preferred_element_type=F32) * xi_ref[h]
                y_ref[rows, hc] = (_rms(o) * _silu(g_ref[rows, hc].astype(F32))).astype(BF16)
                kz = (kp.astype(F32) * zeta_ref[h] * _head_mask(hh, RET_CHUNK, F32)).astype(BF16)
                r_new = r_new + lax.dot_general(kz, vh, _TN, preferred_element_type=F32)
            r_ref[p] = r_new
        return carry

    lax.fori_loop(0, n_chunks, chunk, 0, unroll=8)
    st_ref[...] = r_ref[...]


def _ret_tables(chunk, reps):
    h = N_RET_HEADS
    log_g = jnp.log1p(-(2.0 ** (-5.0 - jnp.arange(h, dtype=F32))))
    idx = jnp.arange(chunk, dtype=F32)
    diff = idx[:, None] - idx[None, :]
    intra = jnp.where(diff >= 0, jnp.exp(jnp.maximum(diff, 0.0) * log_g[:, None, None]), 0.0)
    xi = jnp.exp((idx + 1.0)[None, :] * log_g[:, None])
    zeta = jnp.exp((chunk - 1.0 - idx)[None, :] * log_g[:, None])
    g_chunk = jnp.exp(chunk * log_g)
    eye = jnp.eye(reps, dtype=F32)
    intra_bd = jnp.einsum("ab,hij->haibj", eye, intra).reshape(h, reps * chunk, reps * chunk)
    n = reps * chunk
    xi_t = jnp.broadcast_to(jnp.tile(xi, (1, reps))[:, :, None], (h, n, RET_V_DIM))
    zeta_t = jnp.broadcast_to(jnp.tile(zeta, (1, reps))[:, :, None], (h, n, LANES))
    gdec = jnp.broadcast_to(jnp.repeat(g_chunk, RET_QK_DIM).reshape(h // 2, LANES, 1), (h // 2, LANES, RET_V_DIM))
    return intra_bd, xi_t, zeta_t, gdec


def _ret_prompt(rq, rk, rv, rg, b, s):
    tabs = _ret_tables(RET_CHUNK, 1)
    seq = lambda w: pl.BlockSpec((s, w), lambda i: (i, 0))
    return pl.pallas_call(
        _ret_prompt_kernel,
        grid=(b,),
        in_specs=[seq(RET_QK_WIDTH), seq(RET_QK_WIDTH), seq(RET_V_WIDTH), seq(RET_V_WIDTH)]
                 + [_const_spec(t.shape) for t in tabs],
        out_specs=[seq(RET_V_WIDTH),
                   pl.BlockSpec((None, N_RET_HEADS // 2, LANES, RET_V_DIM), lambda i: (i, 0, 0, 0))],
        out_shape=[jax.ShapeDtypeStruct((b * s, RET_V_WIDTH), BF16),
                   jax.ShapeDtypeStruct((b, N_RET_HEADS // 2, LANES, RET_V_DIM), F32)],
        scratch_shapes=[pltpu.VMEM((N_RET_HEADS // 2, LANES, RET_V_DIM), F32)],
        compiler_params=_cparams("arbitrary"),
        name="ret_prompt",
    )(rq, rk, rv, rg, *tabs)


SAMPLE_GROUP = 32


def _ret_sample_kernel(q_ref, k_ref, v_ref, g_ref, st_ref, intra_ref, xi_ref, zeta_ref, gdec_ref,
                       y_ref, so_ref, *, dec_seq):
    n = q_ref.shape[0]
    nb = n // dec_seq
    tok_seq = lax.broadcasted_iota(jnp.int32, (n, LANES), 0) // dec_seq
    seq_of_col = lax.broadcasted_iota(jnp.int32, (LANES, n), 1) // dec_seq
    for p in range(N_RET_HEADS // 2):
        pc = slice(p * LANES, (p + 1) * LANES)
        qp = q_ref[:, pc]
        kp = k_ref[:, pc]
        st = st_ref[:, p]
        st_flat = st.reshape(nb * LANES, RET_V_DIM).astype(BF16)
        upd = jnp.zeros((nb * LANES, RET_V_DIM), F32)
        for hh in range(2):
            h = 2 * p + hh
            hc = slice(h * RET_V_DIM, (h + 1) * RET_V_DIM)
            qm = qp * _head_mask(hh, n, BF16)
            s = lax.dot_general(qm, kp, _NT, preferred_element_type=F32) * intra_ref[h]
            vh = v_ref[:, hc]
            o = jnp.dot(s.astype(BF16), vh, preferred_element_type=F32)
            qf = qm.astype(F32)
            qexp = jnp.concatenate([jnp.where(tok_seq == b, qf, 0.0) for b in range(nb)], axis=1).astype(BF16)
            o = o + jnp.dot(qexp, st_flat, preferred_element_type=F32) * xi_ref[h]
            y_ref[:, hc] = (_rms(o) * _silu(g_ref[:, hc].astype(F32))).astype(BF16)
            kzt = (kp.astype(F32) * zeta_ref[h] * _head_mask(hh, n, F32)).T
            w = jnp.concatenate([jnp.where(seq_of_col == b, kzt, 0.0) for b in range(nb)], axis=0).astype(BF16)
            upd = upd + jnp.dot(w, vh, preferred_element_type=F32)
        so_ref[:, p] = st * gdec_ref[p] + upd.reshape(nb, LANES, RET_V_DIM)


def _ret_sample(rq, rk, rv, rg, state, dec_seq):
    b = state.shape[0]
    grp = SAMPLE_GROUP
    n = grp * dec_seq
    tabs = _ret_tables(dec_seq, grp)
    tok = lambda w: pl.BlockSpec((n, w), lambda i: (i, 0))
    st_spec = pl.BlockSpec((grp, N_RET_HEADS // 2, LANES, RET_V_DIM), lambda i: (i, 0, 0, 0))
    return pl.pallas_call(
        functools.partial(_ret_sample_kernel, dec_seq=dec_seq),
        grid=(b // grp,),
        in_specs=[tok(RET_QK_WIDTH), tok(RET_QK_WIDTH), tok(RET_V_WIDTH), tok(RET_V_WIDTH), st_spec]
                 + [_const_spec(t.shape) for t in tabs],
        out_specs=[tok(RET_V_WIDTH), st_spec],
        out_shape=[jax.ShapeDtypeStruct((b * dec_seq, RET_V_WIDTH), BF16),
                   jax.ShapeDtypeStruct(state.shape, F32)],
        compiler_params=_cparams("arbitrary"),
        name="ret_sample",
    )(rq, rk, rv, rg, state, *tabs)


def _attn_prompt_kernel(*refs, seq):
    n_pat = len(DILATIONS)
    in_refs = refs[:3 * n_pat]
    o_ref = refs[3 * n_pat]
    acc, m_s, l_s = refs[3 * n_pat + 1:]
    n_blk = seq // BLK

    lo = lax.broadcasted_iota(jnp.int32, (BLK, LANES), 1) < HEAD_DIM
    mask_a, mask_b = _head_mask(0, BLK, BF16), _head_mask(1, BLK, BF16)

    for pi, d in enumerate(DILATIONS):
        q_ref, k_ref, v_ref = in_refs[3 * pi:3 * pi + 3]
        nb = seq // d // BLK
        first, last = pi == 0, pi == n_pat - 1
        assert d == 1 or not last
        n_win = 1 if nb == 1 else 2
        row = lax.broadcasted_iota(jnp.int32, (2 * BLK, n_win * BLK), 0) & (BLK - 1)
        col = lax.broadcasted_iota(jnp.int32, (2 * BLK, n_win * BLK), 1)
        dist = row - col + (n_win - 1) * BLK
        band = (dist >= 0) & (dist <= N_BACK)

        def block(t, carry, d=d, nb=nb, first=first, last=last, q_ref=q_ref, k_ref=k_ref, v_ref=v_ref,
                  n_win=n_win, band=band, col=col):
            i = t % nb
            r = t // nb
            base = pl.multiple_of(t * BLK, BLK)
            cur = pl.ds(base, BLK)
            if n_win == 1:
                valid = band
            else:
                before = pl.ds(pl.multiple_of(jnp.maximum(base - BLK, 0), BLK), BLK)
                valid = band & (col >= jnp.where(i == 0, BLK, 0))
            nat = cur if d == 1 else pl.ds(r + d * BLK * i, BLK, stride=d)
            for p in range(N_PAIRS):
                pc = slice(p * LANES, (p + 1) * LANES)
                qp = q_ref[cur, pc]
                q2 = jnp.concatenate([qp * mask_a, qp * mask_b], axis=0)
                if n_win == 1:
                    kw, vw = k_ref[cur, pc], v_ref[cur, pc]
                else:
                    kw = jnp.concatenate([k_ref[before, pc], k_ref[cur, pc]], axis=0)
                    vw = jnp.concatenate([v_ref[before, pc], v_ref[cur, pc]], axis=0)
                s = lax.dot_general(q2, kw, _NT, preferred_element_type=F32)
                s = jnp.where(valid, s, NEG)
                m = jnp.max(s, axis=-1, keepdims=True)
                e = jnp.exp2(s - m)
                l = jnp.sum(e, axis=-1, keepdims=True)
                o = jnp.dot(e.astype(BF16), vw, preferred_element_type=F32)
                m_p = jnp.where(lo, m[:BLK], m[BLK:])
                l_p = jnp.where(lo, l[:BLK], l[BLK:])
                o_p = jnp.where(lo, o[:BLK], o[BLK:])
                if first:
                    m_s[p, nat, :], l_s[p, nat, :], acc[p, nat, :] = m_p, l_p, o_p
                    continue
                m_o = m_s[p, nat, :]
                m_n = jnp.maximum(m_o, m_p)
                a_o = jnp.exp2(m_o - m_n)
                a_p = jnp.exp2(m_p - m_n)
                l_n = a_o * l_s[p, nat, :] + a_p * l_p
                a_n = a_o * acc[p, nat, :] + a_p * o_p
                if last:
                    o_ref[cur, pc] = (a_n / l_n).astype(BF16)
                else:
                    m_s[p, nat, :], l_s[p, nat, :], acc[p, nat, :] = m_n, l_n, a_n
            return carry

        lax.fori_loop(0, n_blk, block, 0, unroll=4)


def _attn_prompt(qkv_by_dilation, b, s):
    seq_spec = pl.BlockSpec((s, ATT_WIDTH), lambda bi: (bi, 0))
    args = [a.reshape(b * s, ATT_WIDTH) for d in DILATIONS for a in qkv_by_dilation[d]]
    slab = pltpu.VMEM((N_PAIRS, s, LANES), F32)
    return pl.pallas_call(
        functools.partial(_attn_prompt_kernel, seq=s),
        grid=(b,),
        in_specs=[seq_spec] * len(args),
        out_specs=seq_spec,
        out_shape=jax.ShapeDtypeStruct((b * s, ATT_WIDTH), BF16),
        scratch_shapes=[slab, slab, slab],
        compiler_params=_cparams("arbitrary"),
        name="attn_prompt",
    )(*args)


SAMPLE_SEQS_PER_STEP = 2


def _attn_sample_kernel(q_ref, kn_ref, vn_ref, kc_ref, vc_ref, cnt_ref, o_ref, kx, vx, *, buf):
    n_seq, nh, hd, ts = kn_ref.shape
    kx[...] = jnp.zeros(kx.shape, F32)
    vx[...] = jnp.zeros(vx.shape, F32)
    kx[:, :, :, 0:ts] = kn_ref[...]
    vx[:, :, :, 0:ts] = vn_ref[...]
    cnt_c, cnt_n = cnt_ref[:, 0:buf], cnt_ref[:, buf:]
    for sq in range(n_seq):
        q = q_ref[sq]
        k_c, v_c = kc_ref[sq].reshape(nh * hd, buf), vc_ref[sq].reshape(nh * hd, buf)
        k_n, v_n = kx[sq].reshape(nh * hd, LANES), vx[sq].reshape(nh * hd, LANES)
        s_c = jnp.where(cnt_c > 0.0, jnp.dot(q, k_c, preferred_element_type=F32), NEG)
        s_n = jnp.where(cnt_n > 0.0, jnp.dot(q, k_n, preferred_element_type=F32), NEG)
        m = jnp.maximum(jnp.max(s_c, axis=-1, keepdims=True), jnp.max(s_n, axis=-1, keepdims=True))
        e_c = cnt_c * jnp.exp2(s_c - m)
        e_n = cnt_n * jnp.exp2(s_n - m)
        inv = 1.0 / (jnp.sum(e_c, axis=-1, keepdims=True) + jnp.sum(e_n, axis=-1, keepdims=True))
        o_t = (lax.dot_general(v_c, e_c * inv, _NT, preferred_element_type=F32)
               + lax.dot_general(v_n, e_n * inv, _NT, preferred_element_type=F32))
        for h in range(nh):
            o_ref[sq, h] = o_t[h * hd:(h + 1) * hd, h * ts:(h + 1) * ts]


def _attn_sample(q, k_new, v_new, cache_k, cache_v):
    b, nh, ts, hd = q.shape
    buf = cache_k.shape[3]
    dist = buf + np.arange(ts)[:, None] - np.arange(buf + LANES)[None, :]
    cnt = np.zeros(dist.shape, np.float32)
    for d in DILATIONS:
        cnt += (dist >= 0) & (dist <= d * N_BACK) & (dist % d == 0)
    cnt[:, buf + ts:] = 0.0
    cnt = jnp.asarray(np.tile(cnt, (nh, 1)))
    q_rows = (q[:, :, :, None, :] * jnp.eye(nh, dtype=F32)[None, :, None, :, None]).reshape(b, nh * ts, nh * hd)
    n_seq = SAMPLE_SEQS_PER_STEP
    assert b % n_seq == 0
    per_seq = lambda *shape: pl.BlockSpec((n_seq,) + shape, lambda i: (i,) + (0,) * len(shape))
    pad = pltpu.VMEM((n_seq, nh, hd, LANES), F32)
    q = q_rows
    return pl.pallas_call(
        functools.partial(_attn_sample_kernel, buf=buf),
        grid=(b // n_seq,),
        in_specs=[per_seq(nh * ts, nh * hd), per_seq(nh, hd, ts), per_seq(nh, hd, ts),
                  per_seq(nh, hd, buf), per_seq(nh, hd, buf), _const_spec(cnt.shape)],
        out_specs=per_seq(nh, hd, ts),
        out_shape=jax.ShapeDtypeStruct((b, nh, hd, ts), F32),
        scratch_shapes=[pad, pad],
        compiler_params=_cparams("arbitrary"),
        name="attn_sample",
    )(q, k_new, v_new, cache_k, cache_v, cnt)


FF_CHUNK = 256
FFN_SUB = 512


def _ffn_kernel(x_ref, ry_ref, ay_ref, g1_ref, sh2_ref, sc2_ref, g2_ref, n2_ref, fg_ref,
                wo_ref, wg_ref, wu_ref, wd_ref, y_ref):
    tm = x_ref.shape[0]
    n_sub = tm // FFN_SUB if tm % FFN_SUB == 0 else 1
    ts = tm // n_sub
    for sub in range(n_sub):
        rs = slice(sub * ts, (sub + 1) * ts)
        per_row = lambda ref: ref[rs, :] if ref.shape[0] == tm else ref[...]
        mix = jnp.dot(ry_ref[rs, :], wo_ref[0:RET_V_WIDTH, :], preferred_element_type=F32)
        mix = mix + jnp.dot(ay_ref[rs, :], wo_ref[RET_V_WIDTH:, :], preferred_element_type=F32)
        x1 = x_ref[rs, :] + per_row(g1_ref) * mix
        h = _rms(x1) * n2_ref[...]
        h = (h * (1.0 + per_row(sc2_ref)) + per_row(sh2_ref)).astype(BF16)
        ff = jnp.zeros(x1.shape, F32)
        for c in range(D_FF // FF_CHUNK):
            cs = slice(c * FF_CHUNK, (c + 1) * FF_CHUNK)
            gate = jnp.dot(h, wg_ref[:, cs], preferred_element_type=F32)
            up = jnp.dot(h, wu_ref[:, cs], preferred_element_type=F32)
            ff = ff + jnp.dot((_silu(gate) * up).astype(BF16), wd_ref[cs, :], preferred_element_type=F32)
        x2 = x1 + per_row(g2_ref) * ff
        y_ref[rs, :] = _rms(x2) * fg_ref[...]


def _ffn(x2, ret_y, att_y, mod_spec, mod_arr, n2, fg, wo, wg, wu, wd, tm):
    t, d = x2.shape
    row = lambda w: pl.BlockSpec((tm, w), lambda i: (i, 0))
    return pl.pallas_call(
        _ffn_kernel,
        grid=(t // tm,),
        in_specs=[row(d), row(RET_V_WIDTH), row(ATT_WIDTH), mod_spec(2), mod_spec(3), mod_spec(4), mod_spec(5),
                  _const_spec((1, d)), _const_spec((1, d)),
                  _const_spec(wo.shape), _const_spec(wg.shape), _const_spec(wu.shape), _const_spec(wd.shape)],
        out_specs=row(d),
        out_shape=jax.ShapeDtypeStruct((t, d), F32),
        compiler_params=_cparams("arbitrary"),
        name="outproj_ffn",
    )(x2, ret_y, att_y, mod_arr, mod_arr, mod_arr, mod_arr, n2, fg, wo, wg, wu, wd)


def _rope_tables(pos, inv_freq):
    ang = pos.astype(F32)[:, None] * inv_freq[None, :]
    cos, sin = jnp.cos(ang), jnp.sin(ang)
    reps = LANES // HEAD_DIM
    return (jnp.tile(jnp.concatenate([cos, cos], axis=1), (1, reps)),
            jnp.tile(jnp.concatenate([-sin, sin], axis=1), (1, reps)))


def _all_rope_tables(pos):
    half = jnp.arange(0, HEAD_DIM, 2, dtype=F32)
    ret_freq = ROPE_THETA ** (-jnp.linspace(0.0, 1.0, RET_QK_DIM // 2, dtype=F32))
    att_freq = ROPE_THETA ** (-half / HEAD_DIM)
    return _rope_tables(pos, ret_freq) + _rope_tables(pos, att_freq)


def kernel(x_prompt, x_sample, cache_attn_k, cache_attn_v, state_ret, c_prompt, c_sample,
           w_ada, b_ada, norm1_g, w_in, w_out, norm2_g, w_gate, w_up, w_down, final_g):
    b, s, d = x_prompt.shape
    bs, ts, _ = x_sample.shape
    depth = w_in.shape[0]
    assert depth == 1 and s % (BLK * max(DILATIONS)) == 0 and bs % SAMPLE_GROUP == 0

    xp = x_prompt.reshape(b * s, d)
    xs = x_sample.reshape(bs * ts, d)
    fg = final_g.reshape(1, d)
    kp_l, vp_l, rp_l, ks_l, vs_l, rs_l = [], [], [], [], [], []
    for l in range(depth):
        mod = _ada(jnp.concatenate([c_prompt, c_sample], axis=0), w_ada[l], b_ada[l])
        mod_p = mod[:b].reshape(b, 1, N_MOD * d)
        mod_s = jnp.repeat(mod[b:], ts, axis=0)
        w_in_b, wo_b = w_in[l].astype(BF16), w_out[l].astype(BF16)
        wg_b, wu_b, wd_b = w_gate[l].astype(BF16), w_up[l].astype(BF16), w_down[l].astype(BF16)
        g1, n2 = norm1_g[l].reshape(1, d), norm2_g[l].reshape(1, d)

        tm, tm_ffn = 512, 1024
        per_seq = s // tm
        mod_of_seq = lambda n: (lambda k: pl.BlockSpec((None, 1, d), lambda i, k=k: (i // n, 0, k)))
        spec_p = mod_of_seq(per_seq)
        tabs_p = _all_rope_tables(jnp.arange(s))
        outs = _inproj(xp, spec_p, mod_p, g1, w_in_b, tabs_p, tm, lambda i: (i % per_seq, 0), seq=s)
        rq, rk, rv, rg, ak, av = outs[:6]
        qkv = {1: outs[6:9]}
        for di, dd in enumerate(STRIDED):
            qkv[dd] = outs[9 + 3 * di:12 + 3 * di]
        ret_y, ret_state = _ret_prompt(rq, rk, rv, rg, b, s)
        att_y = _attn_prompt(qkv, b, s)
        xp = _ffn(xp, ret_y, att_y, mod_of_seq(s // tm_ffn), mod_p, n2, fg, wo_b, wg_b, wu_b, wd_b, tm_ffn)
        win = min(2048, s)
        to_rows = lambda a: jnp.transpose(a, (0, 3, 1, 2))
        kp_l.append(to_rows(ak)[:, s - win:])
        vp_l.append(to_rows(av)[:, s - win:])
        rp_l.append(ret_state.reshape(b, N_RET_HEADS, RET_QK_DIM, RET_V_DIM))

        n_tok = bs * ts
        tms = min(512, n_tok)
        spec_s = lambda k: pl.BlockSpec((tms, d), lambda i, k=k: (i, k))
        tabs_s = tuple(jnp.tile(tb, (n_tok // ts, 1)) for tb in _all_rope_tables(PAST_LEN + jnp.arange(ts)))
        rq, rk, rv, rg, ak, av, aq = _inproj(xs, spec_s, mod_s, g1, w_in_b, tabs_s, tms, lambda i: (i, 0))
        state2 = state_ret[l].reshape(bs, N_RET_HEADS // 2, LANES, RET_V_DIM)
        ret_y, ret_state = _ret_sample(rq, rk, rv, rg, state2, ts)
        by_seq = lambda a: a.reshape(bs, ts, N_ATT_HEADS, HEAD_DIM)
        cols = lambda a: jnp.transpose(a, (0, 2, 3, 1))
        att_t = _attn_sample(jnp.transpose(by_seq(aq), (0, 2, 1, 3)), cols(by_seq(ak)), cols(by_seq(av)),
                             cols(cache_attn_k[l]), cols(cache_attn_v[l]))
        att_y = to_rows(att_t).reshape(n_tok, ATT_WIDTH).astype(BF16)
        xs = _ffn(xs, ret_y, att_y, spec_s, mod_s, n2, fg, wo_b, wg_b, wu_b, wd_b, tms)
        ks_l.append(by_seq(ak))
        vs_l.append(by_seq(av))
        rs_l.append(ret_state.reshape(bs, N_RET_HEADS, RET_QK_DIM, RET_V_DIM))

    return (xp.reshape(b, s, d), xs.reshape(bs, ts, d),
            jnp.stack(kp_l, 0), jnp.stack(vp_l, 0), jnp.stack(rp_l, 0),
            jnp.stack(ks_l, 0), jnp.stack(vs_l, 0), jnp.stack(rs_l, 0))
```

```python
import functools

import jax
import jax.numpy as jnp
import numpy as np
from jax import lax
from jax.experimental import pallas as pl
from jax.experimental.pallas import tpu as pltpu

D_MODEL = 1024
HEAD_DIM = 64
N_ATT_HEADS = 8
ATT_WIDTH = N_ATT_HEADS * HEAD_DIM
ATT_SCALE = HEAD_DIM ** -0.5
LOG2E = 1.4426950408889634
DILATIONS = (16, 4, 1)
N_BACK = 128
ROPE_THETA = 10000.0
N_RET_HEADS = 4
RET_QK_DIM = 64
RET_V_DIM = 128
RET_QK_WIDTH = N_RET_HEADS * RET_QK_DIM
RET_V_WIDTH = N_RET_HEADS * RET_V_DIM
RET_CHUNK = 128
D_FF = 2816
N_MOD = 6
EPS = 1e-6
PAST_LEN = 8192

LANES = 128
BLK = 128
NEG = -1e30
VMEM_LIMIT = 56 * 1024 * 1024
BF16 = jnp.bfloat16
F32 = jnp.float32

_NT = (((1,), (1,)), ((), ()))
_TN = (((0,), (0,)), ((), ()))


def _cparams(*sem):
    return pltpu.CompilerParams(dimension_semantics=sem, vmem_limit_bytes=VMEM_LIMIT)


def _const_spec(shape):
    n = len(shape)
    return pl.BlockSpec(shape, lambda *_: (0,) * n, pipeline_mode=pl.Buffered(1))


def _head_mask(hh, rows, dtype):
    lane = lax.broadcasted_iota(jnp.int32, (rows, LANES), 1)
    own = (lane >= HEAD_DIM) if hh else (lane < HEAD_DIM)
    return jnp.where(own, 1.0, 0.0).astype(dtype)


def _rms(x):
    return x * lax.rsqrt(jnp.mean(x * x, axis=-1, keepdims=True) + EPS)


def _silu(x):
    return x * jax.nn.sigmoid(x)


def _ada_kernel(c_ref, w_ref, b_ref, o_ref):
    c = _silu(c_ref[...]).astype(BF16)
    o_ref[...] = jnp.dot(c, w_ref[...].astype(BF16), preferred_element_type=F32) + b_ref[...]


def _ada(c_all, w_ada, b_ada):
    n, d = c_all.shape
    width = w_ada.shape[1]
    tn = 1536
    return pl.pallas_call(
        _ada_kernel,
        grid=(width // tn,),
        in_specs=[pl.BlockSpec((n, d), lambda j: (0, 0)),
                  pl.BlockSpec((d, tn), lambda j: (0, j)),
                  pl.BlockSpec((1, tn), lambda j: (0, j))],
        out_specs=pl.BlockSpec((n, tn), lambda j: (0, j)),
        out_shape=jax.ShapeDtypeStruct((n, width), F32),
        compiler_params=_cparams("arbitrary"),
        name="ada_mod",
    )(c_all, w_ada, b_ada.reshape(1, width))


def _rotate(z, cos, sin_signed):
    lane = lax.broadcasted_iota(jnp.int32, (z.shape[0], LANES), 1)
    first_half = (lane & (HEAD_DIM - 1)) < (HEAD_DIM // 2)
    outs = []
    for c in range(z.shape[1] // LANES):
        zc = z[:, c * LANES:(c + 1) * LANES]
        partner = jnp.where(first_half, pltpu.roll(zc, LANES - HEAD_DIM // 2, 1), pltpu.roll(zc, HEAD_DIM // 2, 1))
        outs.append(zc * cos + partner * sin_signed)
    return jnp.concatenate(outs, axis=1)


_IN_OFFS = tuple(int(o) for o in np.cumsum((0, RET_QK_WIDTH, RET_QK_WIDTH, RET_V_WIDTH, RET_V_WIDTH,
                                            ATT_WIDTH, ATT_WIDTH, ATT_WIDTH)))


STRIDED = tuple(sorted(d for d in DILATIONS if d > 1))
STEP = 4
assert all(d == STEP ** (i + 1) for i, d in enumerate(STRIDED))
N_PAIRS = N_ATT_HEADS // 2
INPROJ_SUB = 256
INPROJ_SLAB_SETS = 2


def _inproj_kernel(*refs, prompt):
    x_ref, sh_ref, sc_ref, g_ref, w_ref, cr_ref, sr_ref, ca_ref, sa_ref = refs[:9]
    rq_ref, rk_ref, rv_ref, rg_ref = refs[9:13]
    ak_ref, av_ref = refs[13:15]
    tm = x_ref.shape[0]
    n_sub = tm // INPROJ_SUB if tm % INPROJ_SUB == 0 else 1
    ts = tm // n_sub
    for sub in range(n_sub):
        rs = slice(sub * ts, (sub + 1) * ts)
        per_row = lambda ref: ref[rs, :] if ref.shape[0] == tm else ref[...]
        h = _rms(x_ref[rs, :]) * g_ref[...]
        h = (h * (1.0 + per_row(sc_ref)) + per_row(sh_ref)).astype(BF16)

        def proj(i, h=h):
            return jnp.dot(h, w_ref[:, _IN_OFFS[i]:_IN_OFFS[i + 1]], preferred_element_type=F32)

        cr, sr, ca, sa = cr_ref[rs, :], sr_ref[rs, :], ca_ref[rs, :], sa_ref[rs, :]
        aq = _rotate(proj(4), ca, sa) * (ATT_SCALE * LOG2E)
        ak = _rotate(proj(5), ca, sa)
        av = proj(6)
        if prompt:
            ak_ref[:, :, rs] = ak.T.reshape(N_ATT_HEADS, HEAD_DIM, ts)
            av_ref[:, :, rs] = av.T.reshape(N_ATT_HEADS, HEAD_DIM, ts)
            nat_refs = refs[15:18]
            strided_refs = refs[18:18 + 3 * len(STRIDED)]
            slabs = refs[18 + 3 * len(STRIDED)]
            for a, z in enumerate((aq, ak, av)):
                nat_refs[a][rs, :] = z.astype(BF16)
                for p in range(N_PAIRS):
                    slabs[sub % INPROJ_SLAB_SETS,0, a, p] = z[:, p * LANES:(p + 1) * LANES]
                for lvl, d in enumerate(STRIDED):
                    o_ref = strided_refs[3 * lvl + a]
                    coarse = d // STEP
                    n = ts // d
                    for r_old in range(coarse):
                        for r_new in range(STEP):
                            r = r_new * coarse + r_old
                            for p in range(N_PAIRS):
                                v = slabs[sub % INPROJ_SLAB_SETS,lvl, a, p, pl.ds(r_old * (ts // coarse) + r_new, n, stride=STEP), :]
                                o_ref[r, sub * n:(sub + 1) * n, p * LANES:(p + 1) * LANES] = v.astype(BF16)
                                if lvl + 1 < len(STRIDED):
                                    slabs[sub % INPROJ_SLAB_SETS,lvl + 1, a, p, r * n:(r + 1) * n, :] = v
        else:
            for ref, z in zip((ak_ref, av_ref), (ak, av)):
                ref[rs] = z.reshape(ts, N_ATT_HEADS, HEAD_DIM)
            for ref, z in zip(refs[15:18], (aq, ak, av)):
                ref[rs, :] = z
        rq_ref[rs, :] = _rotate(proj(0), cr, sr).astype(BF16)
        rk_ref[rs, :] = (_rotate(proj(1), cr, sr) * (RET_QK_DIM ** -0.5)).astype(BF16)
        rv_ref[rs, :] = proj(2).astype(BF16)
        rg_ref[rs, :] = proj(3).astype(BF16)


def _inproj(x2, mod_spec, mod_arr, g1, w_in_b, tabs, tm, tab_map, seq=None):
    t, d = x2.shape
    prompt = seq is not None
    row = lambda w: pl.BlockSpec((tm, w), lambda i: (i, 0))
    heads = pl.BlockSpec((tm, N_ATT_HEADS, HEAD_DIM), lambda i: (i, 0, 0))
    tab = pl.BlockSpec((tm, LANES), tab_map)
    heads_shape = jax.ShapeDtypeStruct((t, N_ATT_HEADS, HEAD_DIM), F32)
    out_specs = [row(RET_QK_WIDTH), row(RET_QK_WIDTH), row(RET_V_WIDTH), row(RET_V_WIDTH)]
    out_shape = [jax.ShapeDtypeStruct((t, w), BF16) for w in (RET_QK_WIDTH, RET_QK_WIDTH, RET_V_WIDTH, RET_V_WIDTH)]
    scratch = []
    if prompt:
        per_seq = seq // tm
        by_seq = lambda i: (i // per_seq, 0, 0, i % per_seq)
        out_specs += [pl.BlockSpec((None, N_ATT_HEADS, HEAD_DIM, tm), by_seq)] * 2
        out_shape += [jax.ShapeDtypeStruct((t // seq, N_ATT_HEADS, HEAD_DIM, seq), F32)] * 2
        out_specs += [row(ATT_WIDTH)] * 3
        out_shape += [jax.ShapeDtypeStruct((t, ATT_WIDTH), BF16)] * 3
        for dd in STRIDED:
            spec = pl.BlockSpec((None, dd, tm // dd, ATT_WIDTH), lambda i: (i // per_seq, 0, i % per_seq, 0))
            out_specs += [spec] * 3
            out_shape += [jax.ShapeDtypeStruct((t // seq, dd, seq // dd, ATT_WIDTH), BF16)] * 3
        assert tm % INPROJ_SUB == 0
        scratch = [pltpu.VMEM((INPROJ_SLAB_SETS, len(STRIDED), 3, N_PAIRS, INPROJ_SUB, LANES), F32)]
    else:
        out_specs += [heads] * 2 + [row(ATT_WIDTH)] * 3
        out_shape += [heads_shape] * 2 + [jax.ShapeDtypeStruct((t, ATT_WIDTH), F32)] * 3
    return pl.pallas_call(
        functools.partial(_inproj_kernel, prompt=prompt),
        grid=(t // tm,),
        in_specs=[row(d), mod_spec(0), mod_spec(1), _const_spec((1, d)), _const_spec(w_in_b.shape),
                  tab, tab, tab, tab],
        out_specs=out_specs,
        out_shape=out_shape,
        scratch_shapes=scratch,
        compiler_params=_cparams("arbitrary"),
        name="inproj_rope",
    )(x2, mod_arr, mod_arr, g1, w_in_b, *tabs)


def _ret_prompt_kernel(q_ref, k_ref, v_ref, g_ref, intra_ref, xi_ref, zeta_ref, gdec_ref,
                       y_ref, st_ref, r_ref):
    n_chunks = q_ref.shape[0] // RET_CHUNK
    r_ref[...] = jnp.zeros_like(r_ref)
    masks = (_head_mask(0, RET_CHUNK, BF16), _head_mask(1, RET_CHUNK, BF16))

    def chunk(c, carry):
        rows = pl.ds(pl.multiple_of(c * RET_CHUNK, RET_CHUNK), RET_CHUNK)
        for p in range(N_RET_HEADS // 2):
            pc = slice(p * LANES, (p + 1) * LANES)
            qp = q_ref[rows, pc]
            kp = k_ref[rows, pc]
            r_old = r_ref[p]
            r_new = r_old * gdec_ref[p]
            r_b = r_old.astype(BF16)
            kz = kp.astype(F32) * zeta_ref[p]
            for hh in range(2):
                h = 2 * p + hh
                hc = slice(h * RET_V_DIM, (h + 1) * RET_V_DIM)
                qm = qp * masks[hh]
                s = lax.dot_general(qm, kp, _NT, preferred_element_type=F32) * intra_ref[h]
                vh = v_ref[rows, hc]
                o = jnp.dot(s.astype(BF16), vh, preferred_element_type=F32)
                o = o + jnp.dot(qm, r_b, preferred_element_type=F32) * xi_ref[h]
                y_ref[rows, hc] = (_rms(o) * _silu(g_ref[rows, hc].astype(F32))).astype(BF16)
                kz_h = kz.astype(BF16) * masks[hh]
                r_new = r_new + lax.dot_general(kz_h, vh, _TN, preferred_element_type=F32)
            r_ref[p] = r_new
        return carry

    lax.fori_loop(0, n_chunks, chunk, 0, unroll=8)
    st_ref[...] = r_ref[...]


def _ret_tables(chunk, reps):
    h = N_RET_HEADS
    log_g = jnp.log1p(-(2.0 ** (-5.0 - jnp.arange(h, dtype=F32))))
    idx = jnp.arange(chunk, dtype=F32)
    diff = idx[:, None] - idx[None, :]
    intra = jnp.where(diff >= 0, jnp.exp(jnp.maximum(diff, 0.0) * log_g[:, None, None]), 0.0)
    xi = jnp.exp((idx + 1.0)[None, :] * log_g[:, None])
    zeta = jnp.exp((chunk - 1.0 - idx)[None, :] * log_g[:, None])
    g_chunk = jnp.exp(chunk * log_g)
    eye = jnp.eye(reps, dtype=F32)
    intra_bd = jnp.einsum("ab,hij->haibj", eye, intra).reshape(h, reps * chunk, reps * chunk)
    n = reps * chunk
    xi_t = jnp.broadcast_to(jnp.tile(xi, (1, reps))[:, :, None], (h, n, RET_V_DIM))
    zeta_t = jnp.broadcast_to(jnp.tile(zeta, (1, reps))[:, :, None], (h, n, LANES))
    gdec = jnp.broadcast_to(jnp.repeat(g_chunk, RET_QK_DIM).reshape(h // 2, LANES, 1), (h // 2, LANES, RET_V_DIM))
    return intra_bd, xi_t, zeta_t, gdec


def _ret_prompt(rq, rk, rv, rg, b, s):
    intra, xi, zeta, gdec = _ret_tables(RET_CHUNK, 1)
    zeta_pair = jnp.concatenate([zeta[0::2, :, :RET_QK_DIM], zeta[1::2, :, :RET_QK_DIM]], axis=-1)
    tabs = (intra, xi, zeta_pair, gdec)
    seq = lambda w: pl.BlockSpec((s, w), lambda i: (i, 0))
    return pl.pallas_call(
        _ret_prompt_kernel,
        grid=(b,),
        in_specs=[seq(RET_QK_WIDTH), seq(RET_QK_WIDTH), seq(RET_V_WIDTH), seq(RET_V_WIDTH)]
                 + [_const_spec(t.shape) for t in tabs],
        out_specs=[seq(RET_V_WIDTH),
                   pl.BlockSpec((None, N_RET_HEADS // 2, LANES, RET_V_DIM), lambda i: (i, 0, 0, 0))],
        out_shape=[jax.ShapeDtypeStruct((b * s, RET_V_WIDTH), BF16),
                   jax.ShapeDtypeStruct((b, N_RET_HEADS // 2, LANES, RET_V_DIM), F32)],
        scratch_shapes=[pltpu.VMEM((N_RET_HEADS // 2, LANES, RET_V_DIM), F32)],
        compiler_params=_cparams("arbitrary"),
        name="ret_prompt",
    )(rq, rk, rv, rg, *tabs)


SAMPLE_GROUP = 32


def _ret_sample_kernel(q_ref, k_ref, v_ref, g_ref, st_ref, intra_ref, xi_ref, zeta_ref, gdec_ref,
                       y_ref, so_ref, *, dec_seq):
    n = q_ref.shape[0]
    nb = n // dec_seq
    tok_seq = lax.broadcasted_iota(jnp.int32, (n, LANES), 0) // dec_seq
    seq_of_col = lax.broadcasted_iota(jnp.int32, (LANES, n), 1) // dec_seq
    for p in range(N_RET_HEADS // 2):
        pc = slice(p * LANES, (p + 1) * LANES)
        qp = q_ref[:, pc]
        kp = k_ref[:, pc]
        st = st_ref[:, p]
        st_flat = st.reshape(nb * LANES, RET_V_DIM).astype(BF16)
        upd = jnp.zeros((nb * LANES, RET_V_DIM), F32)
        for hh in range(2):
            h = 2 * p + hh
            hc = slice(h * RET_V_DIM, (h + 1) * RET_V_DIM)
            qm = qp * _head_mask(hh, n, BF16)
            s = lax.dot_general(qm, kp, _NT, preferred_element_type=F32) * intra_ref[h]
            vh = v_ref[:, hc]
            o = jnp.dot(s.astype(BF16), vh, preferred_element_type=F32)
            qf = qm.astype(F32)
            qexp = jnp.concatenate([jnp.where(tok_seq == b, qf, 0.0) for b in range(nb)], axis=1).astype(BF16)
            o = o + jnp.dot(qexp, st_flat, preferred_element_type=F32) * xi_ref[h]
            y_ref[:, hc] = (_rms(o) * _silu(g_ref[:, hc].astype(F32))).astype(BF16)
            kzt = (kp.astype(F32) * zeta_ref[h] * _head_mask(hh, n, F32)).T
            w = jnp.concatenate([jnp.where(seq_of_col == b, kzt, 0.0) for b in range(nb)], axis=0).astype(BF16)
            upd = upd + jnp.dot(w, vh, preferred_element_type=F32)
        so_ref[:, p] = st * gdec_ref[p] + upd.reshape(nb, LANES, RET_V_DIM)


def _ret_sample(rq, rk, rv, rg, state, dec_seq):
    b = state.shape[0]
    grp = SAMPLE_GROUP
    n = grp * dec_seq
    tabs = _ret_tables(dec_seq, grp)
    tok = lambda w: pl.BlockSpec((n, w), lambda i: (i, 0))
    st_spec = pl.BlockSpec((grp, N_RET_HEADS // 2, LANES, RET_V_DIM), lambda i: (i, 0, 0, 0))
    return pl.pallas_call(
        functools.partial(_ret_sample_kernel, dec_seq=dec_seq),
        grid=(b // grp,),
        in_specs=[tok(RET_QK_WIDTH), tok(RET_QK_WIDTH), tok(RET_V_WIDTH), tok(RET_V_WIDTH), st_spec]
                 + [_const_spec(t.shape) for t in tabs],
        out_specs=[tok(RET_V_WIDTH), st_spec],
        out_shape=[jax.ShapeDtypeStruct((b * dec_seq, RET_V_WIDTH), BF16),
                   jax.ShapeDtypeStruct(state.shape, F32)],
        compiler_params=_cparams("arbitrary"),
        name="ret_sample",
    )(rq, rk, rv, rg, state, *tabs)


ATT_UNROLL = 4


def _attn_prompt_kernel(*refs, seq):
    n_pat = len(DILATIONS)
    in_refs = refs[:3 * n_pat]
    o_ref = refs[3 * n_pat]
    acc, m_s, l_s = refs[3 * n_pat + 1:]
    n_blk = seq // BLK

    lo = lax.broadcasted_iota(jnp.int32, (BLK, LANES), 1) < HEAD_DIM
    mask_a, mask_b = _head_mask(0, BLK, BF16), _head_mask(1, BLK, BF16)

    for pi, d in enumerate(DILATIONS):
        q_ref, k_ref, v_ref = in_refs[3 * pi:3 * pi + 3]
        nb = seq // d // BLK
        first, last = pi == 0, pi == n_pat - 1
        assert d == 1 or not last
        static_i = nb <= ATT_UNROLL and ATT_UNROLL % nb == 0
        bands, cols = {}, {}
        for n_win in (1, 2):
            row = lax.broadcasted_iota(jnp.int32, (2 * BLK, n_win * BLK), 0) & (BLK - 1)
            cols[n_win] = lax.broadcasted_iota(jnp.int32, (2 * BLK, n_win * BLK), 1)
            dist = row - cols[n_win] + (n_win - 1) * BLK
            bands[n_win] = (dist >= 0) & (dist <= N_BACK)

        def block(t, u, d=d, nb=nb, first=first, last=last, q_ref=q_ref, k_ref=k_ref, v_ref=v_ref,
                  static_i=static_i, bands=bands, cols=cols):
            i = u % nb if static_i else t % nb
            r = t // nb
            base = pl.multiple_of(t * BLK, BLK)
            cur = pl.ds(base, BLK)
            n_win = 1 if (static_i and i == 0) else 2
            if n_win == 1 or static_i:
                valid = bands[n_win]
            else:
                valid = bands[2] & (cols[2] >= jnp.where(i == 0, BLK, 0))
            if n_win == 2:
                before = pl.ds(pl.multiple_of(jnp.maximum(base - BLK, 0), BLK), BLK)
            nat = cur if d == 1 else pl.ds(r + d * BLK * i, BLK, stride=d)
            for p in range(N_PAIRS):
                pc = slice(p * LANES, (p + 1) * LANES)
                qp = q_ref[cur, pc]
                q2 = jnp.concatenate([qp * mask_a, qp * mask_b], axis=0)
                if n_win == 1:
                    kw, vw = k_ref[cur, pc], v_ref[cur, pc]
                else:
                    kw = jnp.concatenate([k_ref[before, pc], k_ref[cur, pc]], axis=0)
                    vw = jnp.concatenate([v_ref[before, pc], v_ref[cur, pc]], axis=0)
                s = lax.dot_general(q2, kw, _NT, preferred_element_type=F32)
                s = jnp.where(valid, s, NEG)
                m = jnp.max(s, axis=-1, keepdims=True)
                e = jnp.exp2(s - m)
                l = jnp.sum(e, axis=-1, keepdims=True)
                o = jnp.dot(e.astype(BF16), vw, preferred_element_type=F32)
                m_p = jnp.where(lo, m[:BLK], m[BLK:])
                l_p = jnp.where(lo, l[:BLK], l[BLK:])
                o_p = jnp.where(lo, o[:BLK], o[BLK:])
                if first:
                    m_s[p, nat, :], l_s[p, nat, :], acc[p, nat, :] = m_p, l_p, o_p
                    continue
                m_o = m_s[p, nat, :]
                m_n = jnp.maximum(m_o, m_p)
                a_o = jnp.exp2(m_o - m_n)
                a_p = jnp.exp2(m_p - m_n)
                l_n = a_o * l_s[p, nat, :] + a_p * l_p
                a_n = a_o * acc[p, nat, :] + a_p * o_p
                if last:
                    o_ref[cur, pc] = (a_n / l_n).astype(BF16)
                else:
                    m_s[p, nat, :], l_s[p, nat, :], acc[p, nat, :] = m_n, l_n, a_n

        def blocks(g, carry, block=block):
            for u in range(ATT_UNROLL):
                block(g * ATT_UNROLL + u, u)
            return carry

        lax.fori_loop(0, n_blk // ATT_UNROLL, blocks, 0)


def _attn_prompt(qkv_by_dilation, b, s):
    seq_spec = pl.BlockSpec((s, ATT_WIDTH), lambda bi: (bi, 0))
    args = [a.reshape(b * s, ATT_WIDTH) for d in DILATIONS for a in qkv_by_dilation[d]]
    slab = pltpu.VMEM((N_PAIRS, s, LANES), F32)
    return pl.pallas_call(
        functools.partial(_attn_prompt_kernel, seq=s),
        grid=(b,),
        in_specs=[seq_spec] * len(args),
        out_specs=seq_spec,
        out_shape=jax.ShapeDtypeStruct((b * s, ATT_WIDTH), BF16),
        scratch_shapes=[slab, slab, slab],
        compiler_params=_cparams("arbitrary"),
        name="attn_prompt",
    )(*args)


SAMPLE_SEQS_PER_STEP = 2


def _attn_sample_kernel(q_ref, kn_ref, vn_ref, kc_ref, vc_ref, cnt_ref, o_ref, kx, vx, wt, *, buf, ts):
    n_seq, nh, hd, _ = kc_ref.shape
    n_tok, width = q_ref.shape
    assert n_seq == 2 and n_tok == n_seq * ts == 8
    kx[...] = jnp.zeros(kx.shape, F32)
    vx[...] = jnp.zeros(vx.shape, F32)
    wt[...] = jnp.zeros(wt.shape, F32)
    kx[0:n_tok, :] = kn_ref[...]
    vx[0:n_tok, :] = vn_ref[...]
    k_n, v_n = kx[...].T, vx[...].T
    row = lax.broadcasted_iota(jnp.int32, (n_tok, width), 0)
    lane_head = lax.broadcasted_iota(jnp.int32, (n_tok, width), 1) >> (hd.bit_length() - 1)
    first_seq = row < ts
    q8 = q_ref[...]
    q8_swapped = pltpu.roll(q8, ts, 0)
    for sq in range(n_seq):
        q_dup = jnp.where(first_seq, q8, q8_swapped) if sq == 0 else jnp.where(first_seq, q8_swapped, q8)
        q = jnp.concatenate([jnp.where(lane_head == 2 * j + (row >= ts), q_dup, 0.0) for j in range(nh // 2)],
                            axis=0)
        cnt_c, cnt_n = cnt_ref[sq, :, 0:buf], cnt_ref[sq, :, buf:]
        k_c, v_c = kc_ref[sq].reshape(nh * hd, buf), vc_ref[sq].reshape(nh * hd, buf)
        s_c = jnp.where(cnt_c > 0.0, jnp.dot(q, k_c, preferred_element_type=F32), NEG)
        s_n = jnp.where(cnt_n > 0.0, jnp.dot(q, k_n, preferred_element_type=F32), NEG)
        m = jnp.maximum(jnp.max(s_c, axis=-1, keepdims=True), jnp.max(s_n, axis=-1, keepdims=True))
        e_c = cnt_c * jnp.exp2(s_c - m)
        e_n = cnt_n * jnp.exp2(s_n - m)
        inv = 1.0 / (jnp.sum(e_c, axis=-1, keepdims=True) + jnp.sum(e_n, axis=-1, keepdims=True))
        wt[:, sq * nh * ts:(sq + 1) * nh * ts] = (
            lax.dot_general(v_c, e_c * inv, _NT, preferred_element_type=F32)
            + lax.dot_general(v_n, e_n * inv, _NT, preferred_element_type=F32))
    w = wt[...].T
    out = jnp.zeros((n_tok, width), F32)
    for h in range(nh):
        j, odd = h // 2, h % 2
        tile0 = w[2 * ts * j:2 * ts * (j + 1), :]
        tile1 = w[nh * ts + 2 * ts * j:nh * ts + 2 * ts * (j + 1), :]
        if odd:
            rows_h = jnp.where(first_seq, pltpu.roll(tile0, ts, 0), tile1)
        else:
            rows_h = jnp.where(first_seq, tile0, pltpu.roll(tile1, ts, 0))
        out = out + jnp.where(lane_head == h, rows_h, 0.0)
    o_ref[...] = out


def _attn_sample(q, k_new, v_new, cache_k, cache_v, ts):
    n_tok, width = q.shape
    b, nh, hd, buf = cache_k.shape
    n_seq = SAMPLE_SEQS_PER_STEP
    assert b % n_seq == 0 and n_seq * ts <= LANES
    t = np.arange(ts)
    dist_c = buf + t[:, None] - np.arange(buf)[None, :]
    dist_n = t[:, None] - t[None, :]
    count = lambda dist: sum(((dist >= 0) & (dist <= d * N_BACK) & (dist % d == 0)).astype(np.float32)
                             for d in DILATIONS)
    cnt = np.zeros((n_seq, ts, buf + LANES), np.float32)
    cnt[:, :, :buf] = count(dist_c)
    for sq in range(n_seq):
        cnt[sq, :, buf + sq * ts:buf + (sq + 1) * ts] = count(dist_n)
    cnt = jnp.asarray(np.tile(cnt, (1, nh, 1)))
    rows = pl.BlockSpec((n_seq * ts, width), lambda i: (i, 0))
    cache = pl.BlockSpec((n_seq, nh, hd, buf), lambda i: (i, 0, 0, 0))
    pad = pltpu.VMEM((LANES, width), F32)
    return pl.pallas_call(
        functools.partial(_attn_sample_kernel, buf=buf, ts=ts),
        grid=(b // n_seq,),
        in_specs=[rows, rows, rows, cache, cache, _const_spec(cnt.shape)],
        out_specs=rows,
        out_shape=jax.ShapeDtypeStruct((n_tok, width), F32),
        scratch_shapes=[pad, pad, pltpu.VMEM((width, LANES), F32)],
        compiler_params=_cparams("arbitrary"),
        name="attn_sample",
    )(q, k_new, v_new, cache_k, cache_v, cnt)


FF_CHUNK = 256
FFN_SUB = 512


def _ffn_kernel(x_ref, ry_ref, ay_ref, g1_ref, sh2_ref, sc2_ref, g2_ref, n2_ref, fg_ref,
                wo_ref, wg_ref, wu_ref, wd_ref, y_ref):
    tm = x_ref.shape[0]
    n_sub = tm // FFN_SUB if tm % FFN_SUB == 0 else 1
    ts = tm // n_sub
    for sub in range(n_sub):
        rs = slice(sub * ts, (sub + 1) * ts)
        per_row = lambda ref: ref[rs, :] if ref.shape[0] == tm else ref[...]
        mix = jnp.dot(ry_ref[rs, :], wo_ref[0:RET_V_WIDTH, :], preferred_element_type=F32)
        mix = mix + jnp.dot(ay_ref[rs, :], wo_ref[RET_V_WIDTH:, :], preferred_element_type=F32)
        x1 = x_ref[rs, :] + per_row(g1_ref) * mix
        h = _rms(x1) * n2_ref[...]
        h = (h * (1.0 + per_row(sc2_ref)) + per_row(sh2_ref)).astype(BF16)
        ff = jnp.zeros(x1.shape, F32)
        for c in range(D_FF // FF_CHUNK):
            cs = slice(c * FF_CHUNK, (c + 1) * FF_CHUNK)
            gate = jnp.dot(h, wg_ref[:, cs], preferred_element_type=F32)
            up = jnp.dot(h, wu_ref[:, cs], preferred_element_type=F32)
            ff = ff + jnp.dot((_silu(gate) * up).astype(BF16), wd_ref[cs, :], preferred_element_type=F32)
        x2 = x1 + per_row(g2_ref) * ff
        y_ref[rs, :] = _rms(x2) * fg_ref[...]


def _ffn(x2, ret_y, att_y, mod_spec, mod_arr, n2, fg, wo, wg, wu, wd, tm):
    t, d = x2.shape
    row = lambda w: pl.BlockSpec((tm, w), lambda i: (i, 0))
    return pl.pallas_call(
        _ffn_kernel,
        grid=(t // tm,),
        in_specs=[row(d), row(RET_V_WIDTH), row(ATT_WIDTH), mod_spec(2), mod_spec(3), mod_spec(4), mod_spec(5),
                  _const_spec((1, d)), _const_spec((1, d)),
                  _const_spec(wo.shape), _const_spec(wg.shape), _const_spec(wu.shape), _const_spec(wd.shape)],
        out_specs=row(d),
        out_shape=jax.ShapeDtypeStruct((t, d), F32),
        compiler_params=_cparams("arbitrary"),
        name="outproj_ffn",
    )(x2, ret_y, att_y, mod_arr, mod_arr, mod_arr, mod_arr, n2, fg, wo, wg, wu, wd)


def _rope_tables(pos, inv_freq):
    ang = pos.astype(F32)[:, None] * inv_freq[None, :]
    cos, sin = jnp.cos(ang), jnp.sin(ang)
    reps = LANES // HEAD_DIM
    return (jnp.tile(jnp.concatenate([cos, cos], axis=1), (1, reps)),
            jnp.tile(jnp.concatenate([-sin, sin], axis=1), (1, reps)))


def _all_rope_tables(pos):
    half = jnp.arange(0, HEAD_DIM, 2, dtype=F32)
    ret_freq = ROPE_THETA ** (-jnp.linspace(0.0, 1.0, RET_QK_DIM // 2, dtype=F32))
    att_freq = ROPE_THETA ** (-half / HEAD_DIM)
    return _rope_tables(pos, ret_freq) + _rope_tables(pos, att_freq)


def kernel(x_prompt, x_sample, cache_attn_k, cache_attn_v, state_ret, c_prompt, c_sample,
           w_ada, b_ada, norm1_g, w_in, w_out, norm2_g, w_gate, w_up, w_down, final_g):
    b, s, d = x_prompt.shape
    bs, ts, _ = x_sample.shape
    depth = w_in.shape[0]
    assert depth == 1 and s % (BLK * max(DILATIONS)) == 0 and bs % SAMPLE_GROUP == 0

    xp = x_prompt.reshape(b * s, d)
    xs = x_sample.reshape(bs * ts, d)
    fg = final_g.reshape(1, d)
    kp_l, vp_l, rp_l, ks_l, vs_l, rs_l = [], [], [], [], [], []
    for l in range(depth):
        c_rows = jnp.concatenate([c_prompt, jnp.repeat(c_sample, ts, axis=0)], axis=0)
        mod = _ada(c_rows, w_ada[l], b_ada[l])
        mod_p = mod[:b].reshape(b, 1, N_MOD * d)
        mod_s = mod[b:]
        w_in_b, wo_b = w_in[l].astype(BF16), w_out[l].astype(BF16)
        wg_b, wu_b, wd_b = w_gate[l].astype(BF16), w_up[l].astype(BF16), w_down[l].astype(BF16)
        g1, n2 = norm1_g[l].reshape(1, d), norm2_g[l].reshape(1, d)

        tm, tm_ffn = 512, 1024
        per_seq = s // tm
        mod_of_seq = lambda n: (lambda k: pl.BlockSpec((None, 1, d), lambda i, k=k: (i // n, 0, k)))
        spec_p = mod_of_seq(per_seq)
        tabs_p = _all_rope_tables(jnp.arange(s))
        outs = _inproj(xp, spec_p, mod_p, g1, w_in_b, tabs_p, tm, lambda i: (i % per_seq, 0), seq=s)
        rq, rk, rv, rg, ak, av = outs[:6]
        qkv = {1: outs[6:9]}
        for di, dd in enumerate(STRIDED):
            qkv[dd] = outs[9 + 3 * di:12 + 3 * di]
        ret_y, ret_state = _ret_prompt(rq, rk, rv, rg, b, s)
        att_y = _attn_prompt(qkv, b, s)
        xp = _ffn(xp, ret_y, att_y, mod_of_seq(s // tm_ffn), mod_p, n2, fg, wo_b, wg_b, wu_b, wd_b, tm_ffn)
        win = min(2048, s)
        to_rows = lambda a: jnp.transpose(a, (0, 3, 1, 2))
        kp_l.append(to_rows(ak)[:, s - win:])
        vp_l.append(to_rows(av)[:, s - win:])
        rp_l.append(ret_state.reshape(b, N_RET_HEADS, RET_QK_DIM, RET_V_DIM))

        n_tok = bs * ts
        tms = min(512, n_tok)
        spec_s = lambda k: pl.BlockSpec((tms, d), lambda i, k=k: (i, k))
        tabs_s = tuple(jnp.tile(tb, (n_tok // ts, 1)) for tb in _all_rope_tables(PAST_LEN + jnp.arange(ts)))
        rq, rk, rv, rg, ak, av, aq_r, ak_r, av_r = _inproj(xs, spec_s, mod_s, g1, w_in_b, tabs_s, tms,
                                                           lambda i: (i, 0))
        state2 = state_ret[l].reshape(bs, N_RET_HEADS // 2, LANES, RET_V_DIM)
        ret_y, ret_state = _ret_sample(rq, rk, rv, rg, state2, ts)
        by_seq = lambda a: a.reshape(bs, ts, N_ATT_HEADS, HEAD_DIM)
        cols = lambda a: jnp.transpose(a, (0, 2, 3, 1))
        att_y = _attn_sample(aq_r, ak_r, av_r, cols(cache_attn_k[l]), cols(cache_attn_v[l]), ts).astype(BF16)
        xs = _ffn(xs, ret_y, att_y, spec_s, mod_s, n2, fg, wo_b, wg_b, wu_b, wd_b, tms)
        ks_l.append(by_seq(ak))
        vs_l.append(by_seq(av))
        rs_l.append(ret_state.reshape(bs, N_RET_HEADS, RET_QK_DIM, RET_V_DIM))

    return (xp.reshape(b, s, d), xs.reshape(bs, ts, d),
            jnp.stack(kp_l, 0), jnp.stack(vp_l, 0), jnp.stack(rp_l, 0),
            jnp.stack(ks_l, 0), jnp.stack(vs_l, 0), jnp.stack(rs_l, 0))
```
